```python
import jax, jax.numpy as jnp
from jax import lax
import numpy as np

D_MODEL = 2048
BATCH = 1
SEQ = 16384
DEPTH = 4

HEAD_DIM = 128
N_ATTN_HEADS = 8
ATTN_WIDTH = N_ATTN_HEADS * HEAD_DIM
N_CONV_GROUPS = 8
CONV_CH = N_CONV_GROUPS * HEAD_DIM
MIX_WIDTH = ATTN_WIDTH + CONV_CH
IN_COLS = 3 * ATTN_WIDTH + 2 * CONV_CH
CONV_K = 31
MOBA_BLOCK = 256
MOBA_TOPK = 3
Q_CHUNK = 128
D_FF = -(-8 * D_MODEL // (3 * 256)) * 256
N_MOD = 6
RMS_EPS = 1e-6
LN_EPS = 1e-5
NEG_INF = -1e30

kernel_name = 'hybrid_conv_moba_block'


def rms_norm(x, g):
    x32 = x.astype(jnp.float32)
    y = x32 * lax.rsqrt(jnp.mean(x32 * x32, axis=-1, keepdims=True) + RMS_EPS)
    return (y * g.astype(jnp.float32)).astype(x.dtype)


def layer_norm(x, g, b):
    x32 = x.astype(jnp.float32)
    mu = jnp.mean(x32, axis=-1, keepdims=True)
    var = jnp.mean(jnp.square(x32 - mu), axis=-1, keepdims=True)
    y = (x32 - mu) * lax.rsqrt(var + LN_EPS) * g.astype(jnp.float32) + b.astype(jnp.float32)
    return y.astype(x.dtype)


def causal_depthwise_conv(u, w, b):
    k = w.shape[0]
    y = lax.conv_general_dilated(
        u, w[:, None, :].astype(u.dtype), window_strides=(1,),
        padding=((k - 1, 0),), dimension_numbers=('NWC', 'WIO', 'NWC'),
        feature_group_count=u.shape[-1])
    return y + b.astype(u.dtype)


def moba_attention(q, k, v):
    B, H, S, Dh = q.shape
    nb = -(-S // MOBA_BLOCK)
    n_sel = min(MOBA_TOPK, nb)
    pad = nb * MOBA_BLOCK - S
    kb = jnp.pad(k, ((0, 0), (0, 0), (0, pad), (0, 0))).reshape(B, H, nb, MOBA_BLOCK, Dh)
    vb = jnp.pad(v, ((0, 0), (0, 0), (0, pad), (0, 0))).reshape(B, H, nb, MOBA_BLOCK, Dh)
    k_mean = jnp.mean(kb.astype(jnp.float32), axis=3)
    n_chunks = S // Q_CHUNK
    q_chunks = q.reshape(B, H, n_chunks, Q_CHUNK, Dh).transpose(2, 0, 1, 3, 4)
    scale = HEAD_DIM ** -0.5
    b_idx = jnp.arange(B)[:, None, None]
    h_idx = jnp.arange(H)[None, :, None]
    blk_ids = jnp.arange(nb)

    def one_chunk(args):
        qc, ci = args
        q_pos = ci * Q_CHUNK + jnp.arange(Q_CHUNK)
        blk = (ci * Q_CHUNK) // MOBA_BLOCK
        gate = jnp.einsum('bhqd,bhnd->bhqn', qc.astype(jnp.float32), k_mean)
        gate = jnp.where(blk_ids < blk, gate, NEG_INF)
        _, sel = lax.top_k(gate, n_sel)
        k_own = lax.dynamic_index_in_dim(kb, blk, axis=2, keepdims=False)
        v_own = lax.dynamic_index_in_dim(vb, blk, axis=2, keepdims=False)
        k_pos = blk * MOBA_BLOCK + jnp.arange(MOBA_BLOCK)
        s_own = jnp.einsum('bhqd,bhkd->bhqk', qc, k_own,
                           preferred_element_type=jnp.float32) * scale
        s_own = jnp.where(k_pos[None, :] <= q_pos[:, None], s_own, NEG_INF)
        scores = [s_own]
        for j in range(n_sel):
            k_sel = kb[b_idx, h_idx, sel[..., j]]
            s_j = jnp.einsum('bhqd,bhqkd->bhqk', qc, k_sel,
                             preferred_element_type=jnp.float32) * scale
            scores.append(jnp.where(j < blk, s_j, NEG_INF))
        p = jax.nn.softmax(jnp.concatenate(scores, axis=-1), axis=-1).astype(v.dtype)
        p = p.reshape(B, H, Q_CHUNK, n_sel + 1, MOBA_BLOCK)
        out = jnp.einsum('bhqk,bhkd->bhqd', p[..., 0, :], v_own,
                         preferred_element_type=jnp.float32)
        for j in range(n_sel):
            v_sel = vb[b_idx, h_idx, sel[..., j]]
            out = out + jnp.einsum('bhqk,bhqkd->bhqd', p[..., j + 1, :], v_sel,
                                   preferred_element_type=jnp.float32)
        return out.astype(q.dtype)

    out = lax.map(one_chunk, (q_chunks, jnp.arange(n_chunks)))
    return out.transpose(1, 2, 0, 3, 4).reshape(B, H, S, Dh)


def hybrid_mixer(h, w_in, conv_w, conv_b, conv_ln_g, conv_ln_b, w_out):
    B, S, _ = h.shape
    proj = h @ w_in
    q, k, v, a, g = jnp.split(
        proj, [ATTN_WIDTH, 2 * ATTN_WIDTH, 3 * ATTN_WIDTH, 3 * ATTN_WIDTH + CONV_CH], axis=-1)

    def heads(t):
        return t.reshape(B, S, N_ATTN_HEADS, HEAD_DIM).transpose(0, 2, 1, 3)

    attn = moba_attention(heads(q), heads(k), heads(v))
    attn = attn.transpose(0, 2, 1, 3).reshape(B, S, ATTN_WIDTH)
    u = a * jax.nn.sigmoid(g)
    u = causal_depthwise_conv(u, conv_w, conv_b)
    u = jax.nn.silu(layer_norm(u, conv_ln_g, conv_ln_b))
    mix = jnp.concatenate([attn, u], axis=-1)
    return mix @ w_out


def swiglu_ffn(h, w_gate, w_up, w_down):
    return (jax.nn.silu(h @ w_gate) * (h @ w_up)) @ w_down


def setup_inputs(seed: int = 0) -> dict:
    key = jax.random.key(seed)
    ks = jax.random.split(key, 18)

    def nrm(k, shape, s):
        return jax.random.normal(k, shape, jnp.float32) * s

    L, D = DEPTH, D_MODEL
    return {
        'x': nrm(ks[0], (BATCH, SEQ, D), 1.0),
        'c': nrm(ks[1], (BATCH, D), 1.0),
        'ada_w': nrm(ks[2], (L, D, N_MOD * D), 0.5 * D ** -0.5),
        'ada_b': nrm(ks[3], (L, N_MOD * D), 0.02),
        'mix_pre_g': 1.0 + nrm(ks[4], (L, D), 0.02),
        'mix_post_g': 1.0 + nrm(ks[5], (L, D), 0.02),
        'w_in': nrm(ks[6], (L, D, IN_COLS), D ** -0.5),
        'conv_w': nrm(ks[7], (L, CONV_K, CONV_CH), CONV_K ** -0.5),
        'conv_b': nrm(ks[8], (L, CONV_CH), 0.02),
        'conv_ln_g': 1.0 + nrm(ks[9], (L, CONV_CH), 0.02),
        'conv_ln_b': nrm(ks[10], (L, CONV_CH), 0.02),
        'w_out': nrm(ks[11], (L, MIX_WIDTH, D), MIX_WIDTH ** -0.5),
        'ffn_pre_g': 1.0 + nrm(ks[12], (L, D), 0.02),
        'ffn_post_g': 1.0 + nrm(ks[13], (L, D), 0.02),
        'w_gate': nrm(ks[14], (L, D, D_FF), D ** -0.5),
        'w_up': nrm(ks[15], (L, D, D_FF), D ** -0.5),
        'w_down': nrm(ks[16], (L, D_FF, D), D_FF ** -0.5),
    }


def reference(x, c, ada_w, ada_b, mix_pre_g, mix_post_g, w_in, conv_w, conv_b,
              conv_ln_g, conv_ln_b, w_out, ffn_pre_g, ffn_post_g, w_gate, w_up, w_down):
    c_act = jax.nn.silu(c)
    for l in range(DEPTH):
        mod = c_act @ ada_w[l] + ada_b[l]
        sh1, sc1, g1, sh2, sc2, g2 = jnp.split(mod, N_MOD, axis=-1)
        h = rms_norm(x, mix_pre_g[l]) * (1.0 + sc1[:, None, :]) + sh1[:, None, :]
        y = hybrid_mixer(h, w_in[l], conv_w[l], conv_b[l], conv_ln_g[l], conv_ln_b[l], w_out[l])
        x = x + g1[:, None, :] * rms_norm(y, mix_post_g[l])
        h = rms_norm(x, ffn_pre_g[l]) * (1.0 + sc2[:, None, :]) + sh2[:, None, :]
        y = swiglu_ffn(h, w_gate[l], w_up[l], w_down[l])
        x = x + g2[:, None, :] * rms_norm(y, ffn_post_g[l])
    return x
```

```python
import functools
import math

import jax
import jax.numpy as jnp
from jax import lax
from jax.experimental import pallas as pl
from jax.experimental.pallas import tpu as pltpu

D_MODEL = 2048
SEQ = 16384
DEPTH = 4
HEAD_DIM = 128
N_HEADS = 8
ATTN_W = N_HEADS * HEAD_DIM
CONV_CH = 1024
IN_COLS = 3 * ATTN_W + 2 * CONV_CH
CONV_K = 31
MOBA_BLOCK = 256
MOBA_TOPK = 3
N_BLOCKS = SEQ // MOBA_BLOCK
D_FF = 5632
N_MOD = 6
RMS_EPS = 1e-6
LN_EPS = 1e-5

M_INIT = -1e30
MASKED = -2e30

LANE = 128
VMEM_LIMIT = 56 * 1024 * 1024

F32 = jnp.float32
BF16 = jnp.bfloat16

SCALE_LOG2E = (HEAD_DIM ** -0.5) * math.log2(math.e)


def _dot(a, b):
    return jnp.dot(a, b, preferred_element_type=F32)


def _dot_nt(a, b):
    return lax.dot_general(a, b, (((1,), (1,)), ((), ())),
                           preferred_element_type=F32)


def _sigmoid(x):
    return 1.0 / (1.0 + jnp.exp(-x))


def _const_spec(shape):
    return pl.BlockSpec(shape, lambda *_: (0,) * len(shape),
                        pipeline_mode=pl.Buffered(1))


MOD_TN = 1024


def _mod_kernel(c_ref, w_ref, b_ref, o_ref):
    c = c_ref[...]
    ca = c * _sigmoid(c)
    o_ref[...] = jnp.sum(ca * w_ref[...], axis=0, keepdims=True) + b_ref[...]


def _modulation(c, ada_w, ada_b):
    n = N_MOD * D_MODEL
    c_col = c.reshape(D_MODEL, 1)
    out = pl.pallas_call(
        _mod_kernel,
        grid=(DEPTH, n // MOD_TN),
        in_specs=[
            pl.BlockSpec((D_MODEL, 1), lambda l, j: (0, 0)),
            pl.BlockSpec((None, D_MODEL, MOD_TN), lambda l, j: (l, 0, j)),
            pl.BlockSpec((None, 1, MOD_TN), lambda l, j: (l, 0, j)),
        ],
        out_specs=pl.BlockSpec((None, 1, MOD_TN), lambda l, j: (l, 0, j)),
        out_shape=jax.ShapeDtypeStruct((DEPTH, 1, n), F32),
        compiler_params=pltpu.CompilerParams(
            dimension_semantics=("arbitrary", "arbitrary"),
            vmem_limit_bytes=VMEM_LIMIT),
        name="adaln_mod",
    )(c_col, ada_w, ada_b.reshape(DEPTH, 1, n))
    return out.reshape(DEPTH, N_MOD, D_MODEL)


def _norm_mod(x, g, shift, scale):
    ms = jnp.mean(x * x, axis=-1, keepdims=True)
    y = x * lax.rsqrt(ms + RMS_EPS) * g
    return y * (1.0 + scale) + shift


PROJ_TM = 512


def _proj_kernel(x_ref, mod_ref, g_ref, wqk_ref, wvt_ref, wag_ref,
                 q_ref, k_ref, vt_ref, u_ref, km_ref):
    h = _norm_mod(x_ref[...], g_ref[...], mod_ref[0:1, :], mod_ref[1:2, :])
    h = h.astype(BF16)
    q_ref[...] = _dot(h, wqk_ref[:, :ATTN_W])
    k = _dot(h, wqk_ref[:, ATTN_W:])
    k_ref[...] = k.astype(BF16)
    for bi in range(PROJ_TM // MOBA_BLOCK):
        kb = k[bi * MOBA_BLOCK:(bi + 1) * MOBA_BLOCK, :]
        km_ref[bi] = jnp.mean(kb, axis=0, keepdims=True)
    vt = _dot_nt(wvt_ref[...], h).astype(BF16)
    for hd in range(N_HEADS):
        for bi in range(PROJ_TM // MOBA_BLOCK):
            vt_ref[hd, bi] = vt[hd * HEAD_DIM:(hd + 1) * HEAD_DIM,
                                bi * MOBA_BLOCK:(bi + 1) * MOBA_BLOCK]
    a = _dot(h, wag_ref[:, :CONV_CH])
    g = _dot(h, wag_ref[:, CONV_CH:])
    u_ref[...] = a * _sigmoid(g)


def _in_proj(x, mod, pre_g, w_qk, w_vt, w_ag):
    nbt = PROJ_TM // MOBA_BLOCK
    return pl.pallas_call(
        _proj_kernel,
        grid=(SEQ // PROJ_TM,),
        in_specs=[
            pl.BlockSpec((PROJ_TM, D_MODEL), lambda i: (i, 0)),
            _const_spec((N_MOD, D_MODEL)),
            _const_spec((1, D_MODEL)),
            _const_spec((D_MODEL, 2 * ATTN_W)),
            _const_spec((ATTN_W, D_MODEL)),
            _const_spec((D_MODEL, 2 * CONV_CH)),
        ],
        out_specs=[
            pl.BlockSpec((PROJ_TM, ATTN_W), lambda i: (i, 0)),
            pl.BlockSpec((PROJ_TM, ATTN_W), lambda i: (i, 0)),
            pl.BlockSpec((N_HEADS, nbt, HEAD_DIM, MOBA_BLOCK),
                         lambda i: (0, i, 0, 0)),
            pl.BlockSpec((PROJ_TM, CONV_CH), lambda i: (i, 0)),
            pl.BlockSpec((nbt, 1, ATTN_W), lambda i: (i, 0, 0)),
        ],
        out_shape=[
            jax.ShapeDtypeStruct((SEQ, ATTN_W), F32),
            jax.ShapeDtypeStruct((SEQ, ATTN_W), BF16),
            jax.ShapeDtypeStruct((N_HEADS, N_BLOCKS, HEAD_DIM, MOBA_BLOCK),
                                 BF16),
            jax.ShapeDtypeStruct((SEQ, CONV_CH), F32),
            jax.ShapeDtypeStruct((N_BLOCKS, 1, ATTN_W), F32),
        ],
        compiler_params=pltpu.CompilerParams(
            dimension_semantics=("arbitrary",),
            vmem_limit_bytes=VMEM_LIMIT),
        name="in_proj",
    )(x, mod, pre_g, w_qk, w_vt, w_ag)


ATT_TQ = MOBA_BLOCK


def _split_bf16(x):
    hi = x.astype(BF16)
    lo = (x - hi.astype(F32)).astype(BF16)
    return hi, lo


def _attn_kernel(q_ref, k_ref, vt_ref, km_ref, o_ref, qaug_ref):
    blk = pl.program_id(1)
    q32 = q_ref[...]
    qh, ql = _split_bf16(q32)

    km = jnp.concatenate(
        [km_ref[...], jnp.zeros((LANE - N_BLOCKS, HEAD_DIM), F32)], axis=0)
    kh, kl = _split_bf16(km)
    gate = _dot_nt(qh, kh) + (_dot_nt(qh, kl) + _dot_nt(ql, kh))
    lane = lax.broadcasted_iota(jnp.int32, (ATT_TQ, LANE), 1)
    g = jnp.where(lane < blk, gate, M_INIT)
    bias = jnp.full((ATT_TQ, LANE), MASKED, F32)
    for _ in range(MOBA_TOPK):
        m = jnp.max(g, axis=1, keepdims=True)
        idx = jnp.min(jnp.where(g == m, lane, LANE), axis=1, keepdims=True)
        hit = lane == idx
        bias = jnp.where(hit & (m > M_INIT), 0.0, bias)
        g = jnp.where(hit, -jnp.inf, g)
    qaug_ref[:, :HEAD_DIM] = qh
    qaug_ref[:, HEAD_DIM:] = bias.astype(BF16)

    lane_k = lax.broadcasted_iota(jnp.int32, (MOBA_BLOCK, LANE), 1)

    def update(s, vt_blk, carry):
        m, l, acc = carry
        m_new = jnp.maximum(m, jnp.max(s, axis=0, keepdims=True))
        p = jnp.exp2((s - m_new) * SCALE_LOG2E)
        alpha = jnp.exp2((m - m_new) * SCALE_LOG2E)
        l = alpha * l + jnp.sum(p, axis=0, keepdims=True)
        acc = alpha * acc + _dot(vt_blk, p.astype(BF16))
        return m_new, l, acc

    def past_block(j, carry):
        kb = k_ref[pl.ds(pl.multiple_of(j * MOBA_BLOCK, MOBA_BLOCK),
                         MOBA_BLOCK), :]
        onehot = (lane_k == j).astype(BF16)
        kaug = jnp.concatenate([kb, onehot], axis=1)
        s = _dot_nt(kaug, qaug_ref[...])
        return update(s, vt_ref[j], carry)

    init = (jnp.full((1, ATT_TQ), M_INIT, F32),
            jnp.zeros((1, ATT_TQ), F32),
            jnp.zeros((HEAD_DIM, ATT_TQ), F32))
    carry = lax.fori_loop(0, blk, past_block, init)

    kb = k_ref[pl.ds(pl.multiple_of(blk * MOBA_BLOCK, MOBA_BLOCK),
                     MOBA_BLOCK), :]
    s = _dot_nt(kb, qh)
    kpos = lax.broadcasted_iota(jnp.int32, (MOBA_BLOCK, ATT_TQ), 0)
    qpos = lax.broadcasted_iota(jnp.int32, (MOBA_BLOCK, ATT_TQ), 1)
    s = jnp.where(kpos <= qpos, s, MASKED)
    _, l, acc = update(s, vt_ref[blk], carry)
    o_ref[...] = (acc / l).T.astype(BF16)


def _moba_attention(q, k, vt, kmean):
    return pl.pallas_call(
        _attn_kernel,
        grid=(N_HEADS, SEQ // ATT_TQ),
        in_specs=[
            pl.BlockSpec((ATT_TQ, HEAD_DIM), lambda h, t: (t, h)),
            pl.BlockSpec((SEQ, HEAD_DIM), lambda h, t: (0, h)),
            pl.BlockSpec((None, N_BLOCKS, HEAD_DIM, MOBA_BLOCK),
                         lambda h, t: (h, 0, 0, 0)),
            pl.BlockSpec((N_BLOCKS, HEAD_DIM), lambda h, t: (0, h)),
        ],
        out_specs=pl.BlockSpec((ATT_TQ, HEAD_DIM), lambda h, t: (t, h)),
        out_shape=jax.ShapeDtypeStruct((SEQ, ATTN_W), BF16),
        scratch_shapes=[pltpu.VMEM((ATT_TQ, 2 * HEAD_DIM), BF16)],
        compiler_params=pltpu.CompilerParams(
            dimension_semantics=("arbitrary", "arbitrary"),
            vmem_limit_bytes=VMEM_LIMIT),
        name="moba_attn",
    )(q, k, vt, kmean)


CONV_TM = 512
CONV_HALO = 32
CONV_RC = 64
CONV_WIN = CONV_RC + CONV_HALO
LN_RC = 32
SUBLANES = 8


def _conv_kernel(u_ref, halo_ref, w_ref, b_ref, g_ref, beta_ref, o_ref,
                 buf_ref, y_ref):
    i = pl.program_id(0)
    halo = halo_ref[...]
    buf_ref[:CONV_HALO, :] = jnp.where(i > 0, halo, 0.0)
    buf_ref[CONV_HALO:, :] = u_ref[...]
    off = CONV_HALO - (CONV_K - 1)

    def conv_rows(r, _):
        base = pl.multiple_of(r * CONV_RC, CONV_RC)
        for c in range(CONV_CH // LANE):
            cs = slice(c * LANE, (c + 1) * LANE)
            win = buf_ref[pl.ds(base, CONV_WIN), cs]
            acc = jnp.zeros((CONV_RC, LANE), F32)
            for sub in range(SUBLANES):
                shifted = win if sub == 0 else pltpu.roll(
                    win, CONV_WIN - sub, axis=0)
                for j in range(CONV_K):
                    if (off + j) % SUBLANES != sub:
                        continue
                    lo = off + j - sub
                    acc = acc + w_ref[j:j + 1, cs] * shifted[lo:lo + CONV_RC, :]
            y_ref[pl.ds(base, CONV_RC), cs] = acc
        return 0

    lax.fori_loop(0, CONV_TM // CONV_RC, conv_rows, 0)

    def ln_rows(r, _):
        base = pl.multiple_of(r * LN_RC, LN_RC)
        y = y_ref[pl.ds(base, LN_RC), :] + b_ref[...]
        mu = jnp.mean(y, axis=-1, keepdims=True)
        yc = y - mu
        var = jnp.mean(yc * yc, axis=-1, keepdims=True)
        z = yc * lax.rsqrt(var + LN_EPS) * g_ref[...] + beta_ref[...]
        o_ref[pl.ds(base, LN_RC), :] = (z * _sigmoid(z)).astype(BF16)
        return 0

    lax.fori_loop(0, CONV_TM // LN_RC, ln_rows, 0)


def _conv_group(u, conv_w, conv_b, ln_g, ln_b):
    ratio = CONV_TM // CONV_HALO
    return pl.pallas_call(
        _conv_kernel,
        grid=(SEQ // CONV_TM,),
        in_specs=[
            pl.BlockSpec((CONV_TM, CONV_CH), lambda i: (i, 0)),
            pl.BlockSpec((CONV_HALO, CONV_CH),
                         lambda i: (jnp.maximum(i * ratio - 1, 0), 0)),
            _const_spec((CONV_K, CONV_CH)),
            _const_spec((1, CONV_CH)),
            _const_spec((1, CONV_CH)),
            _const_spec((1, CONV_CH)),
        ],
        out_specs=pl.BlockSpec((CONV_TM, CONV_CH), lambda i: (i, 0)),
        out_shape=jax.ShapeDtypeStruct((SEQ, CONV_CH), BF16),
        scratch_shapes=[pltpu.VMEM((CONV_TM + CONV_HALO, CONV_CH), F32),
                        pltpu.VMEM((CONV_TM, CONV_CH), F32)],
        compiler_params=pltpu.CompilerParams(
            dimension_semantics=("arbitrary",),
            vmem_limit_bytes=VMEM_LIMIT),
        name="conv_group",
    )(u, u, conv_w, conv_b, ln_g, ln_b)


OUT_TM = 512


def _rms(y, g):
    ms = jnp.mean(y * y, axis=-1, keepdims=True)
    return y * lax.rsqrt(ms + RMS_EPS) * g


def _out_kernel(x_ref, a_ref, c_ref, w_ref, mod_ref, g_ref, o_ref):
    y = _dot(a_ref[...], w_ref[:ATTN_W, :]) + _dot(c_ref[...], w_ref[ATTN_W:, :])
    o_ref[...] = x_ref[...] + mod_ref[2:3, :] * _rms(y, g_ref[...])


def _out_proj(x, attn, conv, w_out, mod, post_g):
    return pl.pallas_call(
        _out_kernel,
        grid=(SEQ // OUT_TM,),
        in_specs=[
            pl.BlockSpec((OUT_TM, D_MODEL), lambda i: (i, 0)),
            pl.BlockSpec((OUT_TM, ATTN_W), lambda i: (i, 0)),
            pl.BlockSpec((OUT_TM, CONV_CH), lambda i: (i, 0)),
            _const_spec((ATTN_W + CONV_CH, D_MODEL)),
            _const_spec((N_MOD, D_MODEL)),
            _const_spec((1, D_MODEL)),
        ],
        out_specs=pl.BlockSpec((OUT_TM, D_MODEL), lambda i: (i, 0)),
        out_shape=jax.ShapeDtypeStruct((SEQ, D_MODEL), F32),
        compiler_params=pltpu.CompilerParams(
            dimension_semantics=("arbitrary",),
            vmem_limit_bytes=VMEM_LIMIT),
        name="out_proj",
    )(x, attn, conv, w_out, mod, post_g)


FFN_TM = 512
FFN_TF = 512


def _ffn_kernel(x_ref, mod_ref, pre_ref, post_ref, wg_ref, wu_ref, wd_ref,
                o_ref, h_ref, acc_ref):
    f = pl.program_id(1)

    @pl.when(f == 0)
    def _():
        h = _norm_mod(x_ref[...], pre_ref[...], mod_ref[3:4, :], mod_ref[4:5, :])
        h_ref[...] = h.astype(BF16)
        acc_ref[...] = jnp.zeros_like(acc_ref)

    h = h_ref[...]
    g = _dot(h, wg_ref[...])
    u = _dot(h, wu_ref[...])
    a = (g * _sigmoid(g) * u).astype(BF16)
    acc_ref[...] += _dot(a, wd_ref[...])

    @pl.when(f == pl.num_programs(1) - 1)
    def _():
        o_ref[...] = x_ref[...] + mod_ref[5:6, :] * _rms(acc_ref[...],
                                                         post_ref[...])


def _ffn(x, mod, pre_g, post_g, w_gate, w_up, w_down):
    return pl.pallas_call(
        _ffn_kernel,
        grid=(SEQ // FFN_TM, D_FF // FFN_TF),
        in_specs=[
            pl.BlockSpec((FFN_TM, D_MODEL), lambda i, f: (i, 0)),
            _const_spec((N_MOD, D_MODEL)),
            _const_spec((1, D_MODEL)),
            _const_spec((1, D_MODEL)),
            pl.BlockSpec((D_MODEL, FFN_TF), lambda i, f: (0, f)),
            pl.BlockSpec((D_MODEL, FFN_TF), lambda i, f: (0, f)),
            pl.BlockSpec((FFN_TF, D_MODEL), lambda i, f: (f, 0)),
        ],
        out_specs=pl.BlockSpec((FFN_TM, D_MODEL), lambda i, f: (i, 0)),
        out_shape=jax.ShapeDtypeStruct((SEQ, D_MODEL), F32),
        scratch_shapes=[pltpu.VMEM((FFN_TM, D_MODEL), BF16),
                        pltpu.VMEM((FFN_TM, D_MODEL), F32)],
        compiler_params=pltpu.CompilerParams(
            dimension_semantics=("arbitrary", "arbitrary"),
            vmem_limit_bytes=VMEM_LIMIT),
        name="ffn",
    )(x, mod, pre_g, post_g, w_gate, w_up, w_down)


def kernel(x, c, ada_w, ada_b, mix_pre_g, mix_post_g, w_in, conv_w, conv_b,
           conv_ln_g, conv_ln_b, w_out, ffn_pre_g, ffn_post_g, w_gate, w_up,
           w_down):
    assert x.shape == (1, SEQ, D_MODEL) and c.shape == (1, D_MODEL)
    xs = x.reshape(SEQ, D_MODEL)
    mod = _modulation(c, ada_w, ada_b)
    row = lambda v: v.reshape(1, -1)
    for l in range(DEPTH):
        w = w_in[l]
        w_qk = w[:, :2 * ATTN_W].astype(BF16)
        w_vt = w[:, 2 * ATTN_W:3 * ATTN_W].T.astype(BF16)
        w_ag = w[:, 3 * ATTN_W:].astype(BF16)
        q, k, vt, u, kmean = _in_proj(xs, mod[l], row(mix_pre_g[l]),
                                      w_qk, w_vt, w_ag)
        attn = _moba_attention(q, k, vt, kmean.reshape(N_BLOCKS, ATTN_W))
        conv = _conv_group(u, conv_w[l], row(conv_b[l]), row(conv_ln_g[l]),
                           row(conv_ln_b[l]))
        xs = _out_proj(xs, attn, conv, w_out[l].astype(BF16), mod[l],
                       row(mix_post_g[l]))
        xs = _ffn(xs, mod[l], row(ffn_pre_g[l]), row(ffn_post_g[l]),
                  w_gate[l].astype(BF16), w_up[l].astype(BF16),
                  w_down[l].astype(BF16))
    return xs.reshape(1, SEQ, D_MODEL)
```

```python
import functools
import math

import jax
import jax.numpy as jnp
from jax import lax
from jax.experimental import pallas as pl
from jax.experimental.pallas import tpu as pltpu

D_MODEL = 2048
SEQ = 16384
DEPTH = 4
HEAD_DIM = 128
N_HEADS = 8
ATTN_W = N_HEADS * HEAD_DIM
CONV_CH = 1024
IN_COLS = 3 * ATTN_W + 2 * CONV_CH
CONV_K = 31
MOBA_BLOCK = 256
MOBA_TOPK = 3
N_BLOCKS = SEQ // MOBA_BLOCK
D_FF = 5632
N_MOD = 6
RMS_EPS = 1e-6
LN_EPS = 1e-5

M_INIT = -1e30
MASKED = -2e30

LANE = 128
VMEM_LIMIT = 60 * 1024 * 1024

F32 = jnp.float32
BF16 = jnp.bfloat16

SCALE_LOG2E = (HEAD_DIM ** -0.5) * math.log2(math.e)


def _dot(a, b):
    return jnp.dot(a, b, preferred_element_type=F32)


def _dot_nt(a, b):
    return lax.dot_general(a, b, (((1,), (1,)), ((), ())),
                           preferred_element_type=F32)


def _sigmoid(x):
    return 1.0 / (1.0 + jnp.exp(-x))


def _const_spec(shape):
    return pl.BlockSpec(shape, lambda *_: (0,) * len(shape),
                        pipeline_mode=pl.Buffered(1))


MOD_TN = 1024


def _mod_kernel(c_ref, w_ref, b_ref, o_ref):
    c = c_ref[...]
    ca = c * _sigmoid(c)
    o_ref[...] = jnp.sum(ca * w_ref[...], axis=0, keepdims=True) + b_ref[...]


def _modulation(c, ada_w, ada_b):
    n = N_MOD * D_MODEL
    c_col = c.reshape(D_MODEL, 1)
    out = pl.pallas_call(
        _mod_kernel,
        grid=(DEPTH, n // MOD_TN),
        in_specs=[
            pl.BlockSpec((D_MODEL, 1), lambda l, j: (0, 0)),
            pl.BlockSpec((None, D_MODEL, MOD_TN), lambda l, j: (l, 0, j)),
            pl.BlockSpec((None, 1, MOD_TN), lambda l, j: (l, 0, j)),
        ],
        out_specs=pl.BlockSpec((None, 1, MOD_TN), lambda l, j: (l, 0, j)),
        out_shape=jax.ShapeDtypeStruct((DEPTH, 1, n), F32),
        compiler_params=pltpu.CompilerParams(
            dimension_semantics=("arbitrary", "arbitrary"),
            vmem_limit_bytes=VMEM_LIMIT),
        name="adaln_mod",
    )(c_col, ada_w, ada_b.reshape(DEPTH, 1, n))
    return out.reshape(DEPTH, N_MOD, D_MODEL)


def _norm_mod(x, g, shift, scale):
    ms = jnp.mean(x * x, axis=-1, keepdims=True)
    y = x * lax.rsqrt(ms + RMS_EPS) * g
    return y * (1.0 + scale) + shift


PROJ_TM = 512
VT_ROWS = HEAD_DIM + 16


def _proj_kernel(x_ref, mod_ref, g_ref, wqk_ref, wvt_ref, wag_ref,
                 q_ref, k_ref, vt_ref, u_ref, km_ref):
    h = _norm_mod(x_ref[...], g_ref[...], mod_ref[0:1, :], mod_ref[1:2, :])
    h = h.astype(BF16)
    q_ref[...] = _dot(h, wqk_ref[:, :ATTN_W])
    k = _dot(h, wqk_ref[:, ATTN_W:])
    k_ref[...] = k.astype(BF16)
    for bi in range(PROJ_TM // MOBA_BLOCK):
        kb = k[bi * MOBA_BLOCK:(bi + 1) * MOBA_BLOCK, :]
        km_ref[bi] = jnp.mean(kb, axis=0, keepdims=True)
    vt = _dot_nt(wvt_ref[...], h).astype(BF16)
    for hd in range(N_HEADS):
        for bi in range(PROJ_TM // MOBA_BLOCK):
            vt_ref[hd, bi, :HEAD_DIM, :] = vt[
                hd * HEAD_DIM:(hd + 1) * HEAD_DIM,
                bi * MOBA_BLOCK:(bi + 1) * MOBA_BLOCK]
            vt_ref[hd, bi, HEAD_DIM:, :] = jnp.ones(
                (VT_ROWS - HEAD_DIM, MOBA_BLOCK), BF16)
    a = _dot(h, wag_ref[:, :CONV_CH])
    g = _dot(h, wag_ref[:, CONV_CH:])
    u_ref[...] = a * _sigmoid(g)


def _in_proj(x, mod, pre_g, w_qk, w_vt, w_ag):
    nbt = PROJ_TM // MOBA_BLOCK
    return pl.pallas_call(
        _proj_kernel,
        grid=(SEQ // PROJ_TM,),
        in_specs=[
            pl.BlockSpec((PROJ_TM, D_MODEL), lambda i: (i, 0)),
            _const_spec((N_MOD, D_MODEL)),
            _const_spec((1, D_MODEL)),
            _const_spec((D_MODEL, 2 * ATTN_W)),
            _const_spec((ATTN_W, D_MODEL)),
            _const_spec((D_MODEL, 2 * CONV_CH)),
        ],
        out_specs=[
            pl.BlockSpec((PROJ_TM, ATTN_W), lambda i: (i, 0)),
            pl.BlockSpec((PROJ_TM, ATTN_W), lambda i: (i, 0)),
            pl.BlockSpec((N_HEADS, nbt, VT_ROWS, MOBA_BLOCK),
                         lambda i: (0, i, 0, 0)),
            pl.BlockSpec((PROJ_TM, CONV_CH), lambda i: (i, 0)),
            pl.BlockSpec((nbt, 1, ATTN_W), lambda i: (i, 0, 0)),
        ],
        out_shape=[
            jax.ShapeDtypeStruct((SEQ, ATTN_W), F32),
            jax.ShapeDtypeStruct((SEQ, ATTN_W), BF16),
            jax.ShapeDtypeStruct((N_HEADS, N_BLOCKS, VT_ROWS, MOBA_BLOCK),
                                 BF16),
            jax.ShapeDtypeStruct((SEQ, CONV_CH), F32),
            jax.ShapeDtypeStruct((N_BLOCKS, 1, ATTN_W), F32),
        ],
        compiler_params=pltpu.CompilerParams(
            dimension_semantics=("arbitrary",),
            vmem_limit_bytes=VMEM_LIMIT),
        name="in_proj",
    )(x, mod, pre_g, w_qk, w_vt, w_ag)


ATT_QB = 2
ATT_TQ = ATT_QB * MOBA_BLOCK
ATT_G = 4
ATT_NB = 4
ATT_GK = ATT_NB * MOBA_BLOCK


def _split_bf16(x):
    hi = x.astype(BF16)
    lo = (x - hi.astype(F32)).astype(BF16)
    return hi, lo


def _block_bias(q32, km, blk0):
    qh, ql = _split_bf16(q32)
    km = jnp.concatenate(
        [km, jnp.zeros((LANE - N_BLOCKS, HEAD_DIM), F32)], axis=0)
    kh, kl = _split_bf16(km)
    gate = _dot_nt(qh, kh) + (_dot_nt(qh, kl) + _dot_nt(ql, kh))
    lane = lax.broadcasted_iota(jnp.int32, (ATT_TQ, LANE), 1)
    row = lax.broadcasted_iota(jnp.int32, (ATT_TQ, LANE), 0)
    blk = blk0 + row // MOBA_BLOCK
    g = jnp.where(lane < blk, gate, M_INIT)
    bias = jnp.where(lane == blk, 0.0, MASKED)
    for _ in range(MOBA_TOPK):
        m = jnp.max(g, axis=1, keepdims=True)
        idx = jnp.min(jnp.where(g == m, lane, LANE), axis=1, keepdims=True)
        hit = lane == idx
        bias = jnp.where(hit & (m > M_INIT), 0.0, bias)
        g = jnp.where(hit, -jnp.inf, g)
    return qh, bias.astype(BF16)


def _attn_kernel(q_ref, k_ref, vt_ref, km_ref, oh_ref, o_ref, qaug_ref):
    tile = pl.program_id(1)
    blk0 = tile * ATT_QB
    for h in range(ATT_G):
        hs = slice(h * HEAD_DIM, (h + 1) * HEAD_DIM)
        qh, bias = _block_bias(q_ref[:, hs], km_ref[:, hs], blk0)
        qaug_ref[h, :, :HEAD_DIM] = qh
        qaug_ref[h, :, HEAD_DIM:] = bias

    def group(g, carry, causal):
        row0 = pl.multiple_of(g * ATT_GK, ATT_GK)
        onehot = oh_ref[pl.ds(row0, ATT_GK), :]
        if causal:
            kpos = row0 + lax.broadcasted_iota(jnp.int32, (ATT_GK, ATT_TQ), 0)
            qpos = tile * ATT_TQ + lax.broadcasted_iota(
                jnp.int32, (ATT_GK, ATT_TQ), 1)
            keep = kpos <= qpos
        scores = []
        for h in range(ATT_G):
            kb = k_ref[pl.ds(row0, ATT_GK), h * HEAD_DIM:(h + 1) * HEAD_DIM]
            scores.append(
                _dot_nt(jnp.concatenate([kb, onehot], axis=1), qaug_ref[h]))
        out = []
        for h in range(ATT_G):
            m, acc = carry[h]
            s = scores[h]
            if causal:
                s = jnp.where(keep, s, MASKED)
            m_new = jnp.maximum(m, jnp.max(s, axis=0, keepdims=True))
            p = jnp.exp2((s - m_new) * SCALE_LOG2E).astype(BF16)
            alpha = jnp.exp2((m - m_new) * SCALE_LOG2E)
            pv = _dot(vt_ref[h, g * ATT_NB], p[:MOBA_BLOCK])
            for i in range(1, ATT_NB):
                pv = pv + _dot(vt_ref[h, g * ATT_NB + i],
                               p[i * MOBA_BLOCK:(i + 1) * MOBA_BLOCK])
            out.append((m_new, alpha * acc + pv))
        return tuple(out)

    init = tuple((jnp.full((1, ATT_TQ), M_INIT, F32),
                  jnp.zeros((VT_ROWS, ATT_TQ), F32)) for _ in range(ATT_G))
    n_full = blk0 // ATT_NB
    carry = lax.fori_loop(0, n_full, lambda g, c: group(g, c, False), init)
    carry = group(n_full, carry, True)
    for h in range(ATT_G):
        _, acc = carry[h]
        out = acc[:HEAD_DIM] / acc[HEAD_DIM:HEAD_DIM + 1]
        o_ref[:, h * HEAD_DIM:(h + 1) * HEAD_DIM] = out.T.astype(BF16)


def _moba_attention(q, k, vt, kmean, blk_onehot):
    gw = ATT_G * HEAD_DIM
    return pl.pallas_call(
        _attn_kernel,
        grid=(N_HEADS // ATT_G, SEQ // ATT_TQ),
        in_specs=[
            pl.BlockSpec((ATT_TQ, gw), lambda h, t: (t, h)),
            pl.BlockSpec((SEQ, gw), lambda h, t: (0, h),
                         pipeline_mode=pl.Buffered(1)),
            pl.BlockSpec((ATT_G, N_BLOCKS, VT_ROWS, MOBA_BLOCK),
                         lambda h, t: (h, 0, 0, 0),
                         pipeline_mode=pl.Buffered(1)),
            pl.BlockSpec((N_BLOCKS, gw), lambda h, t: (0, h)),
            _const_spec((SEQ, LANE)),
        ],
        out_specs=pl.BlockSpec((ATT_TQ, gw), lambda h, t: (t, h)),
        out_shape=jax.ShapeDtypeStruct((SEQ, ATTN_W), BF16),
        scratch_shapes=[pltpu.VMEM((ATT_G, ATT_TQ, 2 * HEAD_DIM), BF16)],
        compiler_params=pltpu.CompilerParams(
            dimension_semantics=("arbitrary", "arbitrary"),
            vmem_limit_bytes=VMEM_LIMIT),
        name="moba_attn",
    )(q, k, vt, kmean, blk_onehot)


CONV_TM = 512
CONV_HALO = 32
CONV_RC = 64
CONV_WIN = CONV_RC + CONV_HALO
LN_RC = 32
SUBLANES = 8


def _conv_kernel(u_ref, halo_ref, w_ref, b_ref, g_ref, beta_ref, o_ref,
                 buf_ref, y_ref):
    i = pl.program_id(0)
    halo = halo_ref[...]
    buf_ref[:CONV_HALO, :] = jnp.where(i > 0, halo, 0.0)
    buf_ref[CONV_HALO:, :] = u_ref[...]
    off = CONV_HALO - (CONV_K - 1)

    def conv_rows(r, _):
        base = pl.multiple_of(r * CONV_RC, CONV_RC)
        for c in range(CONV_CH // LANE):
            cs = slice(c * LANE, (c + 1) * LANE)
            win = buf_ref[pl.ds(base, CONV_WIN), cs]
            acc = jnp.zeros((CONV_RC, LANE), F32)
            for sub in range(SUBLANES):
                shifted = win if sub == 0 else pltpu.roll(
                    win, CONV_WIN - sub, axis=0)
                for j in range(CONV_K):
                    if (off + j) % SUBLANES != sub:
                        continue
                    lo = off + j - sub
                    acc = acc + w_ref[j:j + 1, cs] * shifted[lo:lo + CONV_RC, :]
            y_ref[pl.ds(base, CONV_RC), cs] = acc
        return 0

    lax.fori_loop(0, CONV_TM // CONV_RC, conv_rows, 0)

    def ln_rows(r, _):
        base = pl.multiple_of(r * LN_RC, LN_RC)
        y = y_ref[pl.ds(base, LN_RC), :] + b_ref[...]
        mu = jnp.mean(y, axis=-1, keepdims=True)
        yc = y - mu
        var = jnp.mean(yc * yc, axis=-1, keepdims=True)
        z = yc * lax.rsqrt(var + LN_EPS) * g_ref[...] + beta_ref[...]
        o_ref[pl.ds(base, LN_RC), :] = (z * _sigmoid(z)).astype(BF16)
        return 0

    lax.fori_loop(0, CONV_TM // LN_RC, ln_rows, 0)


def _conv_group(u, conv_w, conv_b, ln_g, ln_b):
    ratio = CONV_TM // CONV_HALO
    return pl.pallas_call(
        _conv_kernel,
        grid=(SEQ // CONV_TM,),
        in_specs=[
            pl.BlockSpec((CONV_TM, CONV_CH), lambda i: (i, 0)),
            pl.BlockSpec((CONV_HALO, CONV_CH),
                         lambda i: (jnp.maximum(i * ratio - 1, 0), 0)),
            _const_spec((CONV_K, CONV_CH)),
            _const_spec((1, CONV_CH)),
            _const_spec((1, CONV_CH)),
            _const_spec((1, CONV_CH)),
        ],
        out_specs=pl.BlockSpec((CONV_TM, CONV_CH), lambda i: (i, 0)),
        out_shape=jax.ShapeDtypeStruct((SEQ, CONV_CH), BF16),
        scratch_shapes=[pltpu.VMEM((CONV_TM + CONV_HALO, CONV_CH), F32),
                        pltpu.VMEM((CONV_TM, CONV_CH), F32)],
        compiler_params=pltpu.CompilerParams(
            dimension_semantics=("arbitrary",),
            vmem_limit_bytes=VMEM_LIMIT),
        name="conv_group",
    )(u, u, conv_w, conv_b, ln_g, ln_b)


OUT_TM = 512


def _rms(y, g):
    ms = jnp.mean(y * y, axis=-1, keepdims=True)
    return y * lax.rsqrt(ms + RMS_EPS) * g


def _out_kernel(x_ref, a_ref, c_ref, w_ref, mod_ref, g_ref, o_ref):
    y = _dot(a_ref[...], w_ref[:ATTN_W, :]) + _dot(c_ref[...], w_ref[ATTN_W:, :])
    o_ref[...] = x_ref[...] + mod_ref[2:3, :] * _rms(y, g_ref[...])


def _out_proj(x, attn, conv, w_out, mod, post_g):
    return pl.pallas_call(
        _out_kernel,
        grid=(SEQ // OUT_TM,),
        in_specs=[
            pl.BlockSpec((OUT_TM, D_MODEL), lambda i: (i, 0)),
            pl.BlockSpec((OUT_TM, ATTN_W), lambda i: (i, 0)),
            pl.BlockSpec((OUT_TM, CONV_CH), lambda i: (i, 0)),
            _const_spec((ATTN_W + CONV_CH, D_MODEL)),
            _const_spec((N_MOD, D_MODEL)),
            _const_spec((1, D_MODEL)),
        ],
        out_specs=pl.BlockSpec((OUT_TM, D_MODEL), lambda i: (i, 0)),
        out_shape=jax.ShapeDtypeStruct((SEQ, D_MODEL), F32),
        compiler_params=pltpu.CompilerParams(
            dimension_semantics=("arbitrary",),
            vmem_limit_bytes=VMEM_LIMIT),
        name="out_proj",
    )(x, attn, conv, w_out, mod, post_g)


FFN_TM = 512
FFN_TF = 512


def _ffn_kernel(x_ref, mod_ref, pre_ref, post_ref, wg_ref, wu_ref, wd_ref,
                o_ref, h_ref, acc_ref):
    f = pl.program_id(1)

    @pl.when(f == 0)
    def _():
        h = _norm_mod(x_ref[...], pre_ref[...], mod_ref[3:4, :], mod_ref[4:5, :])
        h_ref[...] = h.astype(BF16)
        acc_ref[...] = jnp.zeros_like(acc_ref)

    h = h_ref[...]
    g = _dot(h, wg_ref[...])
    u = _dot(h, wu_ref[...])
    a = (g * _sigmoid(g) * u).astype(BF16)
    acc_ref[...] += _dot(a, wd_ref[...])

    @pl.when(f == pl.num_programs(1) - 1)
    def _():
        o_ref[...] = x_ref[...] + mod_ref[5:6, :] * _rms(acc_ref[...],
                                                         post_ref[...])


def _ffn(x, mod, pre_g, post_g, w_gate, w_up, w_down):
    return pl.pallas_call(
        _ffn_kernel,
        grid=(SEQ // FFN_TM, D_FF // FFN_TF),
        in_specs=[
            pl.BlockSpec((FFN_TM, D_MODEL), lambda i, f: (i, 0)),
            _const_spec((N_MOD, D_MODEL)),
            _const_spec((1, D_MODEL)),
            _const_spec((1, D_MODEL)),
            pl.BlockSpec((D_MODEL, FFN_TF), lambda i, f: (0, f)),
            pl.BlockSpec((D_MODEL, FFN_TF), lambda i, f: (0, f)),
            pl.BlockSpec((FFN_TF, D_MODEL), lambda i, f: (f, 0)),
        ],
        out_specs=pl.BlockSpec((FFN_TM, D_MODEL), lambda i, f: (i, 0)),
        out_shape=jax.ShapeDtypeStruct((SEQ, D_MODEL), F32),
        scratch_shapes=[pltpu.VMEM((FFN_TM, D_MODEL), BF16),
                        pltpu.VMEM((FFN_TM, D_MODEL), F32)],
        compiler_params=pltpu.CompilerParams(
            dimension_semantics=("arbitrary", "arbitrary"),
            vmem_limit_bytes=VMEM_LIMIT),
        name="ffn",
    )(x, mod, pre_g, post_g, w_gate, w_up, w_down)


def kernel(x, c, ada_w, ada_b, mix_pre_g, mix_post_g, w_in, conv_w, conv_b,
           conv_ln_g, conv_ln_b, w_out, ffn_pre_g, ffn_post_g, w_gate, w_up,
           w_down):
    assert x.shape == (1, SEQ, D_MODEL) and c.shape == (1, D_MODEL)
    xs = x.reshape(SEQ, D_MODEL)
    mod = _modulation(c, ada_w, ada_b)
    row = lambda v: v.reshape(1, -1)
    blk_onehot = (jnp.arange(SEQ)[:, None] // MOBA_BLOCK
                  == jnp.arange(LANE)[None, :]).astype(BF16)
    for l in range(DEPTH):
        w = w_in[l]
        w_qk = w[:, :2 * ATTN_W].astype(BF16)
        w_vt = w[:, 2 * ATTN_W:3 * ATTN_W].T.astype(BF16)
        w_ag = w[:, 3 * ATTN_W:].astype(BF16)
        q, k, vt, u, kmean = _in_proj(xs, mod[l], row(mix_pre_g[l]),
                                      w_qk, w_vt, w_ag)
        attn = _moba_attention(q, k, vt, kmean.reshape(N_BLOCKS, ATTN_W),
                               blk_onehot)
        conv = _conv_group(u, conv_w[l], row(conv_b[l]), row(conv_ln_g[l]),
                           row(conv_ln_b[l]))
        xs = _out_proj(xs, attn, conv, w_out[l].astype(BF16), mod[l],
                       row(mix_post_g[l]))
        xs = _ffn(xs, mod[l], row(ffn_pre_g[l]), row(ffn_post_g[l]),
                  w_gate[l].astype(BF16), w_up[l].astype(BF16),
                  w_down[l].astype(BF16))
    return xs.reshape(1, SEQ, D_MODEL)
```

```python
import functools
import math

import jax
import jax.numpy as jnp
from jax import lax
from jax.experimental import pallas as pl
from jax.experimental.pallas import tpu as pltpu

D_MODEL = 2048
SEQ = 16384
DEPTH = 4
HEAD_DIM = 128
N_HEADS = 8
ATTN_W = N_HEADS * HEAD_DIM
CONV_CH = 1024
IN_COLS = 3 * ATTN_W + 2 * CONV_CH
CONV_K = 31
MOBA_BLOCK = 256
MOBA_TOPK = 3
N_BLOCKS = SEQ // MOBA_BLOCK
D_FF = 5632
N_MOD = 6
RMS_EPS = 1e-6
LN_EPS = 1e-5

M_INIT = -1e30
MASKED = -2e30

LANE = 128
VMEM_LIMIT = 60 * 1024 * 1024

F32 = jnp.float32
BF16 = jnp.bfloat16

SCALE_LOG2E = (HEAD_DIM ** -0.5) * math.log2(math.e)


def _dot(a, b):
    return jnp.dot(a, b, preferred_element_type=F32)


def _dot_nt(a, b):
    return lax.dot_general(a, b, (((1,), (1,)), ((), ())),
                           preferred_element_type=F32)


def _sigmoid(x):
    return 1.0 / (1.0 + jnp.exp(-x))


def _const_spec(shape):
    return pl.BlockSpec(shape, lambda *_: (0,) * len(shape),
                        pipeline_mode=pl.Buffered(1))


MOD_TN = 1024


def _mod_kernel(c_ref, w_ref, b_ref, o_ref):
    c = c_ref[...]
    ca = c * _sigmoid(c)
    o_ref[...] = jnp.sum(ca * w_ref[...], axis=0, keepdims=True) + b_ref[...]


def _modulation(c, ada_w, ada_b):
    n = N_MOD * D_MODEL
    c_col = c.reshape(D_MODEL, 1)
    out = pl.pallas_call(
        _mod_kernel,
        grid=(DEPTH, n // MOD_TN),
        in_specs=[
            pl.BlockSpec((D_MODEL, 1), lambda l, j: (0, 0)),
            pl.BlockSpec((None, D_MODEL, MOD_TN), lambda l, j: (l, 0, j)),
            pl.BlockSpec((None, 1, MOD_TN), lambda l, j: (l, 0, j)),
        ],
        out_specs=pl.BlockSpec((None, 1, MOD_TN), lambda l, j: (l, 0, j)),
        out_shape=jax.ShapeDtypeStruct((DEPTH, 1, n), F32),
        compiler_params=pltpu.CompilerParams(
            dimension_semantics=("arbitrary", "arbitrary"),
            vmem_limit_bytes=VMEM_LIMIT),
        name="adaln_mod",
    )(c_col, ada_w, ada_b.reshape(DEPTH, 1, n))
    return out.reshape(DEPTH, N_MOD, D_MODEL)


def _norm_mod(x, g, shift, scale):
    ms = jnp.mean(x * x, axis=-1, keepdims=True)
    y = x * lax.rsqrt(ms + RMS_EPS) * g
    return y * (1.0 + scale) + shift


PROJ_TM = 512
VT_ROWS = HEAD_DIM + 16
ATT_NB = 4
ATT_GK = ATT_NB * MOBA_BLOCK
ATT_NG = SEQ // ATT_GK


def _proj_kernel(x_ref, mod_ref, g_ref, wqk_ref, wvt_ref, wag_ref,
                 q_ref, k_ref, vt_ref, u_ref, km_ref, kn_ref):
    h = _norm_mod(x_ref[...], g_ref[...], mod_ref[0:1, :], mod_ref[1:2, :])
    h = h.astype(BF16)
    q_ref[...] = _dot(h, wqk_ref[:, :ATTN_W])
    k = _dot(h, wqk_ref[:, ATTN_W:])
    k_ref[...] = k.astype(BF16)
    for bi in range(PROJ_TM // MOBA_BLOCK):
        kb = k[bi * MOBA_BLOCK:(bi + 1) * MOBA_BLOCK, :]
        km_ref[bi] = jnp.mean(kb, axis=0, keepdims=True)
        ksq = kb * kb
        norms = []
        for hd in range(N_HEADS):
            n2 = jnp.sum(ksq[:, hd * HEAD_DIM:(hd + 1) * HEAD_DIM], axis=1,
                         keepdims=True)
            norms.append(jnp.broadcast_to(
                jnp.max(n2, axis=0, keepdims=True), (1, HEAD_DIM)))
        kn_ref[bi] = jnp.concatenate(norms, axis=1)
    vt = _dot_nt(wvt_ref[...], h).astype(BF16)
    for hd in range(N_HEADS):
        vt_ref[hd, 0, :HEAD_DIM, :] = vt[hd * HEAD_DIM:(hd + 1) * HEAD_DIM, :]
        vt_ref[hd, 0, HEAD_DIM:, :] = jnp.ones(
            (VT_ROWS - HEAD_DIM, PROJ_TM), BF16)
    a = _dot(h, wag_ref[:, :CONV_CH])
    g = _dot(h, wag_ref[:, CONV_CH:])
    u_ref[...] = a * _sigmoid(g)


def _in_proj(x, mod, pre_g, w_qk, w_vt, w_ag):
    nbt = PROJ_TM // MOBA_BLOCK
    tpg = ATT_GK // PROJ_TM
    return pl.pallas_call(
        _proj_kernel,
        grid=(SEQ // PROJ_TM,),
        in_specs=[
            pl.BlockSpec((PROJ_TM, D_MODEL), lambda i: (i, 0)),
            _const_spec((N_MOD, D_MODEL)),
            _const_spec((1, D_MODEL)),
            _const_spec((D_MODEL, 2 * ATTN_W)),
            _const_spec((ATTN_W, D_MODEL)),
            _const_spec((D_MODEL, 2 * CONV_CH)),
        ],
        out_specs=[
            pl.BlockSpec((PROJ_TM, ATTN_W), lambda i: (i, 0)),
            pl.BlockSpec((PROJ_TM, ATTN_W), lambda i: (i, 0)),
            pl.BlockSpec((N_HEADS, 1, VT_ROWS, PROJ_TM),
                         lambda i: (0, i // tpg, 0, i % tpg)),
            pl.BlockSpec((PROJ_TM, CONV_CH), lambda i: (i, 0)),
            pl.BlockSpec((nbt, 1, ATTN_W), lambda i: (i, 0, 0)),
            pl.BlockSpec((nbt, 1, ATTN_W), lambda i: (i, 0, 0)),
        ],
        out_shape=[
            jax.ShapeDtypeStruct((SEQ, ATTN_W), F32),
            jax.ShapeDtypeStruct((SEQ, ATTN_W), BF16),
            jax.ShapeDtypeStruct((N_HEADS, ATT_NG, VT_ROWS, ATT_GK),
                                 BF16),
            jax.ShapeDtypeStruct((SEQ, CONV_CH), F32),
            jax.ShapeDtypeStruct((N_BLOCKS, 1, ATTN_W), F32),
            jax.ShapeDtypeStruct((N_BLOCKS, 1, ATTN_W), F32),
        ],
        compiler_params=pltpu.CompilerParams(
            dimension_semantics=("arbitrary",),
            vmem_limit_bytes=VMEM_LIMIT),
        name="in_proj",
    )(x, mod, pre_g, w_qk, w_vt, w_ag)


ATT_QB = 2
ATT_TQ = ATT_QB * MOBA_BLOCK
ATT_G = 4
EXP2_SAFE = 100.0


def _split_bf16(x):
    hi = x.astype(BF16)
    lo = (x - hi.astype(F32)).astype(BF16)
    return hi, lo


def _block_bias(q32, km, blk0):
    qh, ql = _split_bf16(q32)
    kh, kl = _split_bf16(km)
    gate = _dot_nt(kh, qh) + (_dot_nt(kl, qh) + _dot_nt(kh, ql))
    cand = lax.broadcasted_iota(jnp.int32, (N_BLOCKS, ATT_TQ), 0)
    qcol = lax.broadcasted_iota(jnp.int32, (N_BLOCKS, ATT_TQ), 1)
    blk = blk0 + qcol // MOBA_BLOCK
    g = jnp.where(cand < blk, gate, M_INIT)
    bias = jnp.where(cand == blk, 0.0, MASKED)
    for _ in range(MOBA_TOPK):
        m = jnp.max(g, axis=0, keepdims=True)
        idx = jnp.min(jnp.where(g == m, cand, N_BLOCKS), axis=0, keepdims=True)
        hit = cand == idx
        bias = jnp.where(hit & (m > M_INIT), 0.0, bias)
        g = jnp.where(hit, -jnp.inf, g)
    bias = jnp.concatenate(
        [bias, jnp.full((LANE - N_BLOCKS, ATT_TQ), MASKED, F32)], axis=0)
    return qh, bias.T.astype(BF16)


def _attn_kernel(q_ref, k_ref, vt_ref, km_ref, kn_ref, o_ref, qaug_ref):
    tile = pl.program_id(1)
    blk0 = tile * ATT_QB
    m_start = []
    score_sq_bound = jnp.float32(0.0)
    visible = lax.broadcasted_iota(
        jnp.int32, (N_BLOCKS, HEAD_DIM), 0) < blk0 + ATT_QB
    for h in range(ATT_G):
        hs = slice(h * HEAD_DIM, (h + 1) * HEAD_DIM)
        q32 = q_ref[:, hs]
        qn2 = jnp.max(jnp.sum(q32 * q32, axis=1, keepdims=True))
        kn2 = jnp.max(jnp.where(visible, kn_ref[:, hs], 0.0))
        score_sq_bound = jnp.maximum(score_sq_bound, qn2 * kn2)
        qh, bias = _block_bias(q32, km_ref[:, hs], blk0)
        qaug_ref[h, :, :HEAD_DIM] = qh
        qaug_ref[h, :, HEAD_DIM:] = bias
        k_own = k_ref[pl.ds(pl.multiple_of(tile * ATT_TQ, ATT_TQ), ATT_TQ), hs]
        prod = (qh.astype(F32) * k_own.astype(F32)).astype(BF16)
        m_start.append(
            _dot_nt(jnp.ones((16, HEAD_DIM), BF16), prod)[0:1, :])

    def scores(g, h, causal):
        row0 = pl.multiple_of(g * ATT_GK, ATT_GK)
        key_blk = g * ATT_NB + lax.broadcasted_iota(
            jnp.int32, (ATT_GK, LANE), 0) // MOBA_BLOCK
        lane = lax.broadcasted_iota(jnp.int32, (ATT_GK, LANE), 1)
        onehot = jnp.where(lane == key_blk, 1.0, 0.0).astype(BF16)
        kb = k_ref[pl.ds(row0, ATT_GK), h * HEAD_DIM:(h + 1) * HEAD_DIM]
        s = _dot_nt(jnp.concatenate([kb, onehot], axis=1), qaug_ref[h])
        if causal:
            kpos = row0 + lax.broadcasted_iota(jnp.int32, (ATT_GK, ATT_TQ), 0)
            qpos = tile * ATT_TQ + lax.broadcasted_iota(
                jnp.int32, (ATT_GK, ATT_TQ), 1)
            s = jnp.where(kpos <= qpos, s, MASKED)
        return s

    def group_one_pass(g, carry, causal):
        out = [None] * ATT_G
        held = None

        def finish(h, m, acc, col_max, p):
            m_new = jnp.maximum(m, col_max)
            beta = jnp.exp2((m - m_new) * SCALE_LOG2E)
            out[h] = (m_new, (acc + _dot(vt_ref[h, g], p)) * beta)

        for h in range(ATT_G):
            m, acc = carry[h]
            s = scores(g, h, causal)
            col_max = jnp.max(s, axis=0, keepdims=True)
            p = jnp.exp2((s - m) * SCALE_LOG2E).astype(BF16)
            if held is not None:
                finish(*held)
            held = (h, m, acc, col_max, p)
        finish(*held)
        return tuple(out)

    def group_two_pass(g, carry, causal):
        out = []
        s_next = scores(g, 0, causal)
        for h in range(ATT_G):
            m, acc = carry[h]
            s = s_next
            if h + 1 < ATT_G:
                s_next = scores(g, h + 1, causal)
            m_new = jnp.maximum(m, jnp.max(s, axis=0, keepdims=True))
            p = jnp.exp2((s - m_new) * SCALE_LOG2E).astype(BF16)
            alpha = jnp.exp2((m - m_new) * SCALE_LOG2E)
            out.append((m_new, alpha * acc + _dot(vt_ref[h, g], p)))
        return tuple(out)

    def run(group):
        init = tuple((m_start[h], jnp.zeros((VT_ROWS, ATT_TQ), F32))
                     for h in range(ATT_G))
        n_full = blk0 // ATT_NB
        carry = lax.fori_loop(0, n_full, lambda g, c: group(g, c, False), init)
        carry = group(n_full, carry, True)
        for h in range(ATT_G):
            _, acc = carry[h]
            out = acc[:HEAD_DIM] / acc[HEAD_DIM:HEAD_DIM + 1]
            o_ref[:, h * HEAD_DIM:(h + 1) * HEAD_DIM] = out.T.astype(BF16)

    limit = (0.5 * EXP2_SAFE / SCALE_LOG2E) ** 2
    bounded = score_sq_bound <= limit
    pl.when(bounded)(lambda: run(group_one_pass))
    pl.when(jnp.logical_not(bounded))(lambda: run(group_two_pass))


def _moba_attention(q, k, vt, kmean, knorm):
    gw = ATT_G * HEAD_DIM
    return pl.pallas_call(
        _attn_kernel,
        grid=(N_HEADS // ATT_G, SEQ // ATT_TQ),
        in_specs=[
            pl.BlockSpec((ATT_TQ, gw), lambda h, t: (t, h)),
            pl.BlockSpec((SEQ, gw), lambda h, t: (0, h),
                         pipeline_mode=pl.Buffered(1)),
            pl.BlockSpec((ATT_G, ATT_NG, VT_ROWS, ATT_GK),
                         lambda h, t: (h, 0, 0, 0),
                         pipeline_mode=pl.Buffered(1)),
            pl.BlockSpec((N_BLOCKS, gw), lambda h, t: (0, h)),
            pl.BlockSpec((N_BLOCKS, gw), lambda h, t: (0, h)),
        ],
        out_specs=pl.BlockSpec((ATT_TQ, gw), lambda h, t: (t, h)),
        out_shape=jax.ShapeDtypeStruct((SEQ, ATTN_W), BF16),
        scratch_shapes=[pltpu.VMEM((ATT_G, ATT_TQ, 2 * HEAD_DIM), BF16)],
        compiler_params=pltpu.CompilerParams(
            dimension_semantics=("arbitrary", "arbitrary"),
            vmem_limit_bytes=VMEM_LIMIT),
        name="moba_attn",
    )(q, k, vt, kmean, knorm)


CONV_TM = 512
CONV_HALO = 32
CONV_RC = 64
CONV_WIN = CONV_RC + CONV_HALO
LN_RC = 128
SUBLANES = 8


def _conv_kernel(u_ref, halo_ref, w_ref, b_ref, g_ref, beta_ref, o_ref,
                 buf_ref, y_ref):
    i = pl.program_id(0)
    halo = halo_ref[...]
    buf_ref[:CONV_HALO, :] = jnp.where(i > 0, halo, 0.0)
    buf_ref[CONV_HALO:, :] = u_ref[...]
    off = CONV_HALO - (CONV_K - 1)

    def conv_rows(r, _):
        base = pl.multiple_of(r * CONV_RC, CONV_RC)
        for c in range(CONV_CH // LANE):
            cs = slice(c * LANE, (c + 1) * LANE)
            win = buf_ref[pl.ds(base, CONV_WIN), cs]
            acc = jnp.zeros((CONV_RC, LANE), F32)
            for sub in range(SUBLANES):
                shifted = win if sub == 0 else pltpu.roll(
                    win, CONV_WIN - sub, axis=0)
                for j in range(CONV_K):
                    if (off + j) % SUBLANES != sub:
                        continue
                    lo = off + j - sub
                    acc = acc + w_ref[j:j + 1, cs] * shifted[lo:lo + CONV_RC, :]
            y_ref[pl.ds(base, CONV_RC), cs] = acc
        return 0

    lax.fori_loop(0, CONV_TM // CONV_RC, conv_rows, 0)

    def ln_rows(r, _):
        base = pl.multiple_of(r * LN_RC, LN_RC)
        y = y_ref[pl.ds(base, LN_RC), :] + b_ref[...]
        mu = jnp.mean(y, axis=-1, keepdims=True)
        yc = y - mu
        var = jnp.mean(yc * yc, axis=-1, keepdims=True)
        z = yc * lax.rsqrt(var + LN_EPS) * g_ref[...] + beta_ref[...]
        o_ref[pl.ds(base, LN_RC), :] = (z * _sigmoid(z)).astype(BF16)
        return 0

    lax.fori_loop(0, CONV_TM // LN_RC, ln_rows, 0)


def _conv_group(u, conv_w, conv_b, ln_g, ln_b):
    ratio = CONV_TM // CONV_HALO
    return pl.pallas_call(
        _conv_kernel,
        grid=(SEQ // CONV_TM,),
        in_specs=[
            pl.BlockSpec((CONV_TM, CONV_CH), lambda i: (i, 0)),
            pl.BlockSpec((CONV_HALO, CONV_CH),
                         lambda i: (jnp.maximum(i * ratio - 1, 0), 0)),
            _const_spec((CONV_K, CONV_CH)),
            _const_spec((1, CONV_CH)),
            _const_spec((1, CONV_CH)),
            _const_spec((1, CONV_CH)),
        ],
        out_specs=pl.BlockSpec((CONV_TM, CONV_CH), lambda i: (i, 0)),
        out_shape=jax.ShapeDtypeStruct((SEQ, CONV_CH), BF16),
        scratch_shapes=[pltpu.VMEM((CONV_TM + CONV_HALO, CONV_CH), F32),
                        pltpu.VMEM((CONV_TM, CONV_CH), F32)],
        compiler_params=pltpu.CompilerParams(
            dimension_semantics=("arbitrary",),
            vmem_limit_bytes=VMEM_LIMIT),
        name="conv_group",
    )(u, u, conv_w, conv_b, ln_g, ln_b)


OUT_TM = 512


def _rms(y, g):
    ms = jnp.mean(y * y, axis=-1, keepdims=True)
    return y * lax.rsqrt(ms + RMS_EPS) * g


def _out_kernel(x_ref, a_ref, c_ref, w_ref, mod_ref, g_ref, o_ref):
    y = _dot(a_ref[...], w_ref[:ATTN_W, :]) + _dot(c_ref[...], w_ref[ATTN_W:, :])
    o_ref[...] = x_ref[...] + mod_ref[2:3, :] * _rms(y, g_ref[...])


def _out_proj(x, attn, conv, w_out, mod, post_g):
    return pl.pallas_call(
        _out_kernel,
        grid=(SEQ // OUT_TM,),
        in_specs=[
            pl.BlockSpec((OUT_TM, D_MODEL), lambda i: (i, 0)),
            pl.BlockSpec((OUT_TM, ATTN_W), lambda i: (i, 0)),
            pl.BlockSpec((OUT_TM, CONV_CH), lambda i: (i, 0)),
            _const_spec((ATTN_W + CONV_CH, D_MODEL)),
            _const_spec((N_MOD, D_MODEL)),
            _const_spec((1, D_MODEL)),
        ],
        out_specs=pl.BlockSpec((OUT_TM, D_MODEL), lambda i: (i, 0)),
        out_shape=jax.ShapeDtypeStruct((SEQ, D_MODEL), F32),
        compiler_params=pltpu.CompilerParams(
            dimension_semantics=("arbitrary",),
            vmem_limit_bytes=VMEM_LIMIT),
        name="out_proj",
    )(x, attn, conv, w_out, mod, post_g)


FFN_TM = 512
FFN_TF = 512


def _ffn_kernel(x_ref, mod_ref, pre_ref, post_ref, wg_ref, wu_ref, wd_ref,
                o_ref, h_ref, acc_ref):
    f = pl.program_id(1)
    last = pl.num_programs(1) - 1

    def chunk(h):
        g = _dot(h, wg_ref[...])
        u = _dot(h, wu_ref[...])
        a = (g * _sigmoid(g) * u).astype(BF16)
        return _dot(a, wd_ref[...])

    @pl.when(f == 0)
    def _():
        h = _norm_mod(x_ref[...], pre_ref[...], mod_ref[3:4, :], mod_ref[4:5, :])
        h = h.astype(BF16)
        h_ref[...] = h
        acc_ref[...] = chunk(h)

    @pl.when((f > 0) & (f < last))
    def _():
        acc_ref[...] += chunk(h_ref[...])

    @pl.when(f == last)
    def _():
        y = acc_ref[...] + chunk(h_ref[...])
        o_ref[...] = x_ref[...] + mod_ref[5:6, :] * _rms(y, post_ref[...])


def _ffn(x, mod, pre_g, post_g, w_gate, w_up, w_down):
    return pl.pallas_call(
        _ffn_kernel,
        grid=(SEQ // FFN_TM, D_FF // FFN_TF),
        in_specs=[
            pl.BlockSpec((FFN_TM, D_MODEL), lambda i, f: (i, 0)),
            _const_spec((N_MOD, D_MODEL)),
            _const_spec((1, D_MODEL)),
            _const_spec((1, D_MODEL)),
            pl.BlockSpec((D_MODEL, FFN_TF), lambda i, f: (0, f)),
            pl.BlockSpec((D_MODEL, FFN_TF), lambda i, f: (0, f)),
            pl.BlockSpec((FFN_TF, D_MODEL), lambda i, f: (f, 0)),
        ],
        out_specs=pl.BlockSpec((FFN_TM, D_MODEL), lambda i, f: (i, 0)),
        out_shape=jax.ShapeDtypeStruct((SEQ, D_MODEL), F32),
        scratch_shapes=[pltpu.VMEM((FFN_TM, D_MODEL), BF16),
                        pltpu.VMEM((FFN_TM, D_MODEL), F32)],
        compiler_params=pltpu.CompilerParams(
            dimension_semantics=("arbitrary", "arbitrary"),
            vmem_limit_bytes=VMEM_LIMIT),
        name="ffn",
    )(x, mod, pre_g, post_g, w_gate, w_up, w_down)


def kernel(x, c, ada_w, ada_b, mix_pre_g, mix_post_g, w_in, conv_w, conv_b,
           conv_ln_g, conv_ln_b, w_out, ffn_pre_g, ffn_post_g, w_gate, w_up,
           w_down):
    assert x.shape == (1, SEQ, D_MODEL) and c.shape == (1, D_MODEL)
    xs = x.reshape(SEQ, D_MODEL)
    mod = _modulation(c, ada_w, ada_b)
    row = lambda v: v.reshape(1, -1)
    for l in range(DEPTH):
        w = w_in[l]
        w_qk = w[:, :2 * ATTN_W].astype(BF16)
        w_vt = w[:, 2 * ATTN_W:3 * ATTN_W].T.astype(BF16)
        w_ag = w[:, 3 * ATTN_W:].astype(BF16)
        q, k, vt, u, kmean, knorm = _in_proj(xs, mod[l], row(mix_pre_g[l]),
                                             w_qk, w_vt, w_ag)
        attn = _moba_attention(q, k, vt, kmean.reshape(N_BLOCKS, ATTN_W),
                               knorm.reshape(N_BLOCKS, ATTN_W))
        conv = _conv_group(u, conv_w[l], row(conv_b[l]), row(conv_ln_g[l]),
                           row(conv_ln_b[l]))
        xs = _out_proj(xs, attn, conv, w_out[l].astype(BF16), mod[l],
                       row(mix_post_g[l]))
        xs = _ffn(xs, mod[l], row(ffn_pre_g[l]), row(ffn_post_g[l]),
                  w_gate[l].astype(BF16), w_up[l].astype(BF16),
                  w_down[l].astype(BF16))
    return xs.reshape(1, SEQ, D_MODEL)
```

```python
import functools
import math

import jax
import jax.numpy as jnp
from jax import lax
from jax.experimental import pallas as pl
from jax.experimental.pallas import tpu as pltpu

D_MODEL = 2048
SEQ = 16384
DEPTH = 4
HEAD_DIM = 128
N_HEADS = 8
ATTN_W = N_HEADS * HEAD_DIM
CONV_CH = 1024
IN_COLS = 3 * ATTN_W + 2 * CONV_CH
CONV_K = 31
MOBA_BLOCK = 256
MOBA_TOPK = 3
N_BLOCKS = SEQ // MOBA_BLOCK
D_FF = 5632
N_MOD = 6
RMS_EPS = 1e-6
LN_EPS = 1e-5

M_INIT = -1e30
MASKED = -2e30

LANE = 128
VMEM_LIMIT = 60 * 1024 * 1024

F32 = jnp.float32
BF16 = jnp.bfloat16

SCALE_LOG2E = (HEAD_DIM ** -0.5) * math.log2(math.e)


def _dot(a, b):
    return jnp.dot(a, b, preferred_element_type=F32)


def _dot_nt(a, b):
    return lax.dot_general(a, b, (((1,), (1,)), ((), ())),
                           preferred_element_type=F32)


def _sigmoid(x):
    return 1.0 / (1.0 + jnp.exp(-x))


def _const_spec(shape):
    return pl.BlockSpec(shape, lambda *_: (0,) * len(shape),
                        pipeline_mode=pl.Buffered(1))


MOD_TN = 1024


def _mod_kernel(c_ref, w_ref, b_ref, o_ref):
    c = c_ref[...]
    ca = c * _sigmoid(c)
    o_ref[...] = jnp.sum(ca * w_ref[...], axis=0, keepdims=True) + b_ref[...]


def _modulation(c, ada_w, ada_b):
    n = N_MOD * D_MODEL
    c_col = c.reshape(D_MODEL, 1)
    out = pl.pallas_call(
        _mod_kernel,
        grid=(DEPTH, n // MOD_TN),
        in_specs=[
            pl.BlockSpec((D_MODEL, 1), lambda l, j: (0, 0)),
            pl.BlockSpec((None, D_MODEL, MOD_TN), lambda l, j: (l, 0, j)),
            pl.BlockSpec((None, 1, MOD_TN), lambda l, j: (l, 0, j)),
        ],
        out_specs=pl.BlockSpec((None, 1, MOD_TN), lambda l, j: (l, 0, j)),
        out_shape=jax.ShapeDtypeStruct((DEPTH, 1, n), F32),
        compiler_params=pltpu.CompilerParams(
            dimension_semantics=("arbitrary", "arbitrary"),
            vmem_limit_bytes=VMEM_LIMIT),
        name="adaln_mod",
    )(c_col, ada_w, ada_b.reshape(DEPTH, 1, n))
    return out.reshape(DEPTH, N_MOD, D_MODEL)


def _norm_mod(x, g, shift, scale):
    ms = jnp.mean(x * x, axis=-1, keepdims=True)
    y = x * lax.rsqrt(ms + RMS_EPS) * g
    return y * (1.0 + scale) + shift


PROJ_TM = 512
VT_ROWS = HEAD_DIM + 16
ATT_NB = 4
ATT_GK = ATT_NB * MOBA_BLOCK
ATT_NG = SEQ // ATT_GK


def _proj_kernel(x_ref, mod_ref, g_ref, wqk_ref, wvt_ref, wag_ref,
                 q_ref, k_ref, vt_ref, u_ref, km_ref, kn_ref):
    h = _norm_mod(x_ref[...], g_ref[...], mod_ref[0:1, :], mod_ref[1:2, :])
    h = h.astype(BF16)
    q_ref[...] = _dot(h, wqk_ref[:, :ATTN_W])
    k = _dot(h, wqk_ref[:, ATTN_W:])
    k_ref[...] = k.astype(BF16)
    for bi in range(PROJ_TM // MOBA_BLOCK):
        kb = k[bi * MOBA_BLOCK:(bi + 1) * MOBA_BLOCK, :]
        km_ref[bi] = jnp.mean(kb, axis=0, keepdims=True)
        ksq = kb * kb
        norms = []
        for hd in range(N_HEADS):
            n2 = jnp.sum(ksq[:, hd * HEAD_DIM:(hd + 1) * HEAD_DIM], axis=1,
                         keepdims=True)
            norms.append(jnp.broadcast_to(
                jnp.max(n2, axis=0, keepdims=True), (1, HEAD_DIM)))
        kn_ref[bi] = jnp.concatenate(norms, axis=1)
    vt = _dot_nt(wvt_ref[...], h).astype(BF16)
    for hd in range(N_HEADS):
        vt_ref[hd, 0, :HEAD_DIM, :] = vt[hd * HEAD_DIM:(hd + 1) * HEAD_DIM, :]
        vt_ref[hd, 0, HEAD_DIM:, :] = jnp.ones(
            (VT_ROWS - HEAD_DIM, PROJ_TM), BF16)
    a = _dot(h, wag_ref[:, :CONV_CH])
    g = _dot(h, wag_ref[:, CONV_CH:])
    u_ref[...] = a * _sigmoid(g)


def _in_proj(x, mod, pre_g, w_qk, w_vt, w_ag):
    nbt = PROJ_TM // MOBA_BLOCK
    tpg = ATT_GK // PROJ_TM
    return pl.pallas_call(
        _proj_kernel,
        grid=(SEQ // PROJ_TM,),
        in_specs=[
            pl.BlockSpec((PROJ_TM, D_MODEL), lambda i: (i, 0)),
            _const_spec((N_MOD, D_MODEL)),
            _const_spec((1, D_MODEL)),
            _const_spec((D_MODEL, 2 * ATTN_W)),
            _const_spec((ATTN_W, D_MODEL)),
            _const_spec((D_MODEL, 2 * CONV_CH)),
        ],
        out_specs=[
            pl.BlockSpec((PROJ_TM, ATTN_W), lambda i: (i, 0)),
            pl.BlockSpec((PROJ_TM, ATTN_W), lambda i: (i, 0)),
            pl.BlockSpec((N_HEADS, 1, VT_ROWS, PROJ_TM),
                         lambda i: (0, i // tpg, 0, i % tpg)),
            pl.BlockSpec((PROJ_TM, CONV_CH), lambda i: (i, 0)),
            pl.BlockSpec((nbt, 1, ATTN_W), lambda i: (i, 0, 0)),
            pl.BlockSpec((nbt, 1, ATTN_W), lambda i: (i, 0, 0)),
        ],
        out_shape=[
            jax.ShapeDtypeStruct((SEQ, ATTN_W), F32),
            jax.ShapeDtypeStruct((SEQ, ATTN_W), BF16),
            jax.ShapeDtypeStruct((N_HEADS, ATT_NG, VT_ROWS, ATT_GK),
                                 BF16),
            jax.ShapeDtypeStruct((SEQ, CONV_CH), F32),
            jax.ShapeDtypeStruct((N_BLOCKS, 1, ATTN_W), F32),
            jax.ShapeDtypeStruct((N_BLOCKS, 1, ATTN_W), F32),
        ],
        compiler_params=pltpu.CompilerParams(
            dimension_semantics=("arbitrary",),
            vmem_limit_bytes=VMEM_LIMIT),
        name="in_proj",
    )(x, mod, pre_g, w_qk, w_vt, w_ag)


ATT_QB = 2
ATT_TQ = ATT_QB * MOBA_BLOCK
ATT_G = 4
EXP2_SAFE = 80.0


def _split_bf16(x):
    hi = x.astype(BF16)
    lo = (x - hi.astype(F32)).astype(BF16)
    return hi, lo


def _block_bias(qh, ql, km, blk0, m_ref):
    kh, kl = _split_bf16(km)
    gate = _dot_nt(kh, qh) + (_dot_nt(kl, qh) + _dot_nt(kh, ql))
    cand = lax.broadcasted_iota(jnp.int32, (N_BLOCKS, ATT_TQ), 0)
    qcol = lax.broadcasted_iota(jnp.int32, (N_BLOCKS, ATT_TQ), 1)
    blk = blk0 + qcol // MOBA_BLOCK
    g = jnp.where(cand < blk, gate, M_INIT)
    bias = jnp.where(cand == blk, 0.0, MASKED)
    for _ in range(MOBA_TOPK):
        m = jnp.max(g, axis=0, keepdims=True)
        idx = jnp.min(jnp.where(g == m, cand, N_BLOCKS), axis=0, keepdims=True)
        hit = cand == idx
        bias = jnp.where(hit & (m > M_INIT), 0.0, bias)
        g = jnp.where(hit, -jnp.inf, g)
    trow = lax.broadcasted_iota(jnp.int32, (LANE - N_BLOCKS, ATT_TQ), 0)
    tail = jnp.where(trow == 0, -m_ref, 0.0)
    return jnp.concatenate([bias, tail], axis=0).T.astype(BF16)


def _attn_kernel(q_ref, k_ref, vt_ref, km_ref, kn_ref, o_ref, qaug_ref):
    tile = pl.program_id(1)
    blk0 = tile * ATT_QB
    score_sq_bound = jnp.float32(0.0)
    visible = lax.broadcasted_iota(
        jnp.int32, (N_BLOCKS, HEAD_DIM), 0) < blk0 + ATT_QB
    for h in range(ATT_G):
        hs = slice(h * HEAD_DIM, (h + 1) * HEAD_DIM)
        q32 = q_ref[:, hs]
        qn2 = jnp.max(jnp.sum(q32 * q32, axis=1, keepdims=True))
        kn2 = jnp.max(jnp.where(visible, kn_ref[:, hs], 0.0))
        score_sq_bound = jnp.maximum(score_sq_bound, qn2 * kn2)
        qh, ql = _split_bf16(q32)
        k_own = k_ref[pl.ds(pl.multiple_of(tile * ATT_TQ, ATT_TQ), ATT_TQ), hs]
        prod = (qh.astype(F32) * k_own.astype(F32)).astype(BF16)
        m_ref = _dot_nt(jnp.ones((16, HEAD_DIM), BF16), prod)[0:1, :]
        qaug_ref[h, :, :HEAD_DIM] = qh
        qaug_ref[h, :, HEAD_DIM:] = _block_bias(qh, ql, km_ref[:, hs], blk0,
                                                m_ref)

    def scores(g, h, causal):
        row0 = pl.multiple_of(g * ATT_GK, ATT_GK)
        key_blk = g * ATT_NB + lax.broadcasted_iota(
            jnp.int32, (ATT_GK, LANE), 0) // MOBA_BLOCK
        lane = lax.broadcasted_iota(jnp.int32, (ATT_GK, LANE), 1)
        extra = jnp.where((lane == key_blk) | (lane == N_BLOCKS),
                          1.0, 0.0).astype(BF16)
        kb = k_ref[pl.ds(row0, ATT_GK), h * HEAD_DIM:(h + 1) * HEAD_DIM]
        s = _dot_nt(jnp.concatenate([kb, extra], axis=1), qaug_ref[h])
        if causal:
            kpos = row0 + lax.broadcasted_iota(jnp.int32, (ATT_GK, ATT_TQ), 0)
            qpos = tile * ATT_TQ + lax.broadcasted_iota(
                jnp.int32, (ATT_GK, ATT_TQ), 1)
            s = jnp.where(kpos <= qpos, s, MASKED)
        return s

    def finish(accs):
        for h in range(ATT_G):
            out = accs[h][:HEAD_DIM] / accs[h][HEAD_DIM:HEAD_DIM + 1]
            o_ref[:, h * HEAD_DIM:(h + 1) * HEAD_DIM] = out.T.astype(BF16)

    n_full = blk0 // ATT_NB

    def run_bounded():
        def group(g, accs, causal):
            out = [None] * ATT_G
            held = None
            for h in range(ATT_G):
                p = jnp.exp2(scores(g, h, causal) * SCALE_LOG2E).astype(BF16)
                if held is not None:
                    out[held[0]] = accs[held[0]] + _dot(vt_ref[held[0], g],
                                                        held[1])
                held = (h, p)
            out[held[0]] = accs[held[0]] + _dot(vt_ref[held[0], g], held[1])
            return tuple(out)

        def pair(i, accs):
            return group(2 * i + 1, group(2 * i, accs, False), False)

        init = tuple(jnp.zeros((VT_ROWS, ATT_TQ), F32) for _ in range(ATT_G))
        accs = lax.fori_loop(0, n_full // 2, pair, init)
        odd = n_full % 2 == 1

        @pl.when(odd)
        def _():
            finish(group(n_full, group(n_full - 1, accs, False), True))

        @pl.when(jnp.logical_not(odd))
        def _():
            finish(group(n_full, accs, True))

    def run_unbounded():
        def group(g, carry, causal):
            out = []
            s_next = scores(g, 0, causal)
            for h in range(ATT_G):
                m, acc = carry[h]
                s = s_next
                if h + 1 < ATT_G:
                    s_next = scores(g, h + 1, causal)
                m_new = jnp.maximum(m, jnp.max(s, axis=0, keepdims=True))
                p = jnp.exp2((s - m_new) * SCALE_LOG2E).astype(BF16)
                alpha = jnp.exp2((m - m_new) * SCALE_LOG2E)
                out.append((m_new, alpha * acc + _dot(vt_ref[h, g], p)))
            return tuple(out)

        init = tuple((jnp.full((1, ATT_TQ), M_INIT, F32),
                      jnp.zeros((VT_ROWS, ATT_TQ), F32)) for _ in range(ATT_G))
        carry = lax.fori_loop(0, n_full, lambda g, c: group(g, c, False), init)
        carry = group(n_full, carry, True)
        finish([acc for _, acc in carry])

    limit = (0.5 * EXP2_SAFE / SCALE_LOG2E) ** 2
    bounded = score_sq_bound <= limit
    pl.when(bounded)(run_bounded)
    pl.when(jnp.logical_not(bounded))(run_unbounded)


def _moba_attention(q, k, vt, kmean, knorm):
    gw = ATT_G * HEAD_DIM
    return pl.pallas_call(
        _attn_kernel,
        grid=(N_HEADS // ATT_G, SEQ // ATT_TQ),
        in_specs=[
            pl.BlockSpec((ATT_TQ, gw), lambda h, t: (t, h)),
            pl.BlockSpec((SEQ, gw), lambda h, t: (0, h),
                         pipeline_mode=pl.Buffered(1)),
            pl.BlockSpec((ATT_G, ATT_NG, VT_ROWS, ATT_GK),
                         lambda h, t: (h, 0, 0, 0),
                         pipeline_mode=pl.Buffered(1)),
            pl.BlockSpec((N_BLOCKS, gw), lambda h, t: (0, h)),
            pl.BlockSpec((N_BLOCKS, gw), lambda h, t: (0, h)),
        ],
        out_specs=pl.BlockSpec((ATT_TQ, gw), lambda h, t: (t, h)),
        out_shape=jax.ShapeDtypeStruct((SEQ, ATTN_W), BF16),
        scratch_shapes=[pltpu.VMEM((ATT_G, ATT_TQ, 2 * HEAD_DIM), BF16)],
        compiler_params=pltpu.CompilerParams(
            dimension_semantics=("arbitrary", "arbitrary"),
            vmem_limit_bytes=VMEM_LIMIT),
        name="moba_attn",
    )(q, k, vt, kmean, knorm)


CONV_TM = 512
CONV_HALO = 32
CONV_RC = 64
CONV_WIN = CONV_RC + CONV_HALO
LN_RC = 128
SUBLANES = 8


def _conv_kernel(u_ref, halo_ref, w_ref, b_ref, g_ref, beta_ref, o_ref,
                 buf_ref, y_ref):
    i = pl.program_id(0)
    halo = halo_ref[...]
    buf_ref[:CONV_HALO, :] = jnp.where(i > 0, halo, 0.0)
    buf_ref[CONV_HALO:, :] = u_ref[...]
    off = CONV_HALO - (CONV_K - 1)

    def conv_rows(r, _):
        base = pl.multiple_of(r * CONV_RC, CONV_RC)
        for c in range(CONV_CH // LANE):
            cs = slice(c * LANE, (c + 1) * LANE)
            win = buf_ref[pl.ds(base, CONV_WIN), cs]
            acc = jnp.zeros((CONV_RC, LANE), F32)
            for sub in range(SUBLANES):
                shifted = win if sub == 0 else pltpu.roll(
                    win, CONV_WIN - sub, axis=0)
                for j in range(CONV_K):
                    if (off + j) % SUBLANES != sub:
                        continue
                    lo = off + j - sub
                    acc = acc + w_ref[j:j + 1, cs] * shifted[lo:lo + CONV_RC, :]
            y_ref[pl.ds(base, CONV_RC), cs] = acc
        return 0

    lax.fori_loop(0, CONV_TM // CONV_RC, conv_rows, 0)

    def ln_rows(r, _):
        base = pl.multiple_of(r * LN_RC, LN_RC)
        y = y_ref[pl.ds(base, LN_RC), :] + b_ref[...]
        mu = jnp.mean(y, axis=-1, keepdims=True)
        yc = y - mu
        var = jnp.mean(yc * yc, axis=-1, keepdims=True)
        z = yc * lax.rsqrt(var + LN_EPS) * g_ref[...] + beta_ref[...]
        o_ref[pl.ds(base, LN_RC), :] = (z * _sigmoid(z)).astype(BF16)
        return 0

    lax.fori_loop(0, CONV_TM // LN_RC, ln_rows, 0)


def _conv_group(u, conv_w, conv_b, ln_g, ln_b):
    ratio = CONV_TM // CONV_HALO
    return pl.pallas_call(
        _conv_kernel,
        grid=(SEQ // CONV_TM,),
        in_specs=[
            pl.BlockSpec((CONV_TM, CONV_CH), lambda i: (i, 0)),
            pl.BlockSpec((CONV_HALO, CONV_CH),
                         lambda i: (jnp.maximum(i * ratio - 1, 0), 0)),
            _const_spec((CONV_K, CONV_CH)),
            _const_spec((1, CONV_CH)),
            _const_spec((1, CONV_CH)),
            _const_spec((1, CONV_CH)),
        ],
        out_specs=pl.BlockSpec((CONV_TM, CONV_CH), lambda i: (i, 0)),
        out_shape=jax.ShapeDtypeStruct((SEQ, CONV_CH), BF16),
        scratch_shapes=[pltpu.VMEM((CONV_TM + CONV_HALO, CONV_CH), F32),
                        pltpu.VMEM((CONV_TM, CONV_CH), F32)],
        compiler_params=pltpu.CompilerParams(
            dimension_semantics=("arbitrary",),
            vmem_limit_bytes=VMEM_LIMIT),
        name="conv_group",
    )(u, u, conv_w, conv_b, ln_g, ln_b)


OUT_TM = 512


def _rms(y, g):
    ms = jnp.mean(y * y, axis=-1, keepdims=True)
    return y * lax.rsqrt(ms + RMS_EPS) * g


def _out_kernel(x_ref, a_ref, c_ref, w_ref, mod_ref, g_ref, o_ref):
    y = _dot(a_ref[...], w_ref[:ATTN_W, :]) + _dot(c_ref[...], w_ref[ATTN_W:, :])
    o_ref[...] = x_ref[...] + mod_ref[2:3, :] * _rms(y, g_ref[...])


def _out_proj(x, attn, conv, w_out, mod, post_g):
    return pl.pallas_call(
        _out_kernel,
        grid=(SEQ // OUT_TM,),
        in_specs=[
            pl.BlockSpec((OUT_TM, D_MODEL), lambda i: (i, 0)),
            pl.BlockSpec((OUT_TM, ATTN_W), lambda i: (i, 0)),
            pl.BlockSpec((OUT_TM, CONV_CH), lambda i: (i, 0)),
            _const_spec((ATTN_W + CONV_CH, D_MODEL)),
            _const_spec((N_MOD, D_MODEL)),
            _const_spec((1, D_MODEL)),
        ],
        out_specs=pl.BlockSpec((OUT_TM, D_MODEL), lambda i: (i, 0)),
        out_shape=jax.ShapeDtypeStruct((SEQ, D_MODEL), F32),
        compiler_params=pltpu.CompilerParams(
            dimension_semantics=("arbitrary",),
            vmem_limit_bytes=VMEM_LIMIT),
        name="out_proj",
    )(x, attn, conv, w_out, mod, post_g)


FFN_TM = 512
FFN_TF = 512


def _ffn_kernel(x_ref, mod_ref, pre_ref, post_ref, wg_ref, wu_ref, wd_ref,
                o_ref, h_ref, acc_ref):
    f = pl.program_id(1)
    last = pl.num_programs(1) - 1

    def chunk(h):
        g = _dot(h, wg_ref[...])
        u = _dot(h, wu_ref[...])
        a = (g * _sigmoid(g) * u).astype(BF16)
        return _dot(a, wd_ref[...])

    @pl.when(f == 0)
    def _():
        h = _norm_mod(x_ref[...], pre_ref[...], mod_ref[3:4, :], mod_ref[4:5, :])
        h = h.astype(BF16)
        h_ref[...] = h
        acc_ref[...] = chunk(h)

    @pl.when((f > 0) & (f < last))
    def _():
        acc_ref[...] += chunk(h_ref[...])

    @pl.when(f == last)
    def _():
        y = acc_ref[...] + chunk(h_ref[...])
        o_ref[...] = x_ref[...] + mod_ref[5:6, :] * _rms(y, post_ref[...])


def _ffn(x, mod, pre_g, post_g, w_gate, w_up, w_down):
    return pl.pallas_call(
        _ffn_kernel,
        grid=(SEQ // FFN_TM, D_FF // FFN_TF),
        in_specs=[
            pl.BlockSpec((FFN_TM, D_MODEL), lambda i, f: (i, 0)),
            _const_spec((N_MOD, D_MODEL)),
            _const_spec((1, D_MODEL)),
            _const_spec((1, D_MODEL)),
            pl.BlockSpec((D_MODEL, FFN_TF), lambda i, f: (0, f)),
            pl.BlockSpec((D_MODEL, FFN_TF), lambda i, f: (0, f)),
            pl.BlockSpec((FFN_TF, D_MODEL), lambda i, f: (f, 0)),
        ],
        out_specs=pl.BlockSpec((FFN_TM, D_MODEL), lambda i, f: (i, 0)),
        out_shape=jax.ShapeDtypeStruct((SEQ, D_MODEL), F32),
        scratch_shapes=[pltpu.VMEM((FFN_TM, D_MODEL), BF16),
                        pltpu.VMEM((FFN_TM, D_MODEL), F32)],
        compiler_params=pltpu.CompilerParams(
            dimension_semantics=("arbitrary", "arbitrary"),
            vmem_limit_bytes=VMEM_LIMIT),
        name="ffn",
    )(x, mod, pre_g, post_g, w_gate, w_up, w_down)


def kernel(x, c, ada_w, ada_b, mix_pre_g, mix_post_g, w_in, conv_w, conv_b,
           conv_ln_g, conv_ln_b, w_out, ffn_pre_g, ffn_post_g, w_gate, w_up,
           w_down):
    assert x.shape == (1, SEQ, D_MODEL) and c.shape == (1, D_MODEL)
    xs = x.reshape(SEQ, D_MODEL)
    mod = _modulation(c, ada_w, ada_b)
    row = lambda v: v.reshape(1, -1)
    for l in range(DEPTH):
        w = w_in[l]
        w_qk = w[:, :2 * ATTN_W].astype(BF16)
        w_vt = w[:, 2 * ATTN_W:3 * ATTN_W].T.astype(BF16)
        w_ag = w[:, 3 * ATTN_W:].astype(BF16)
        q, k, vt, u, kmean, knorm = _in_proj(xs, mod[l], row(mix_pre_g[l]),
                                             w_qk, w_vt, w_ag)
        attn = _moba_attention(q, k, vt, kmean.reshape(N_BLOCKS, ATTN_W),
                               knorm.reshape(N_BLOCKS, ATTN_W))
        conv = _conv_group(u, conv_w[l], row(conv_b[l]), row(conv_ln_g[l]),
                           row(conv_ln_b[l]))
        xs = _out_proj(xs, attn, conv, w_out[l].astype(BF16), mod[l],
                       row(mix_post_g[l]))
        xs = _ffn(xs, mod[l], row(ffn_pre_g[l]), row(ffn_post_g[l]),
                  w_gate[l].astype(BF16), w_up[l].astype(BF16),
                  w_down[l].astype(BF16))
    return xs.reshape(1, SEQ, D_MODEL)
```

```python
import functools
import math

import jax
import jax.numpy as jnp
from jax import lax
from jax.experimental import pallas as pl
from jax.experimental.pallas import tpu as pltpu

D_MODEL = 2048
SEQ = 16384
DEPTH = 4
HEAD_DIM = 128
N_HEADS = 8
ATTN_W = N_HEADS * HEAD_DIM
CONV_CH = 1024
IN_COLS = 3 * ATTN_W + 2 * CONV_CH
CONV_K = 31
MOBA_BLOCK = 256
MOBA_TOPK = 3
N_BLOCKS = SEQ // MOBA_BLOCK
D_FF = 5632
N_MOD = 6
RMS_EPS = 1e-6
LN_EPS = 1e-5

M_INIT = -1e30
MASKED = -2e30

LANE = 128
VMEM_LIMIT = 60 * 1024 * 1024

F32 = jnp.float32
BF16 = jnp.bfloat16

SCALE_LOG2E = (HEAD_DIM ** -0.5) * math.log2(math.e)


def _dot(a, b):
    return jnp.dot(a, b, preferred_element_type=F32)


def _dot_nt(a, b):
    return lax.dot_general(a, b, (((1,), (1,)), ((), ())),
                           preferred_element_type=F32)


def _sigmoid(x):
    return 1.0 / (1.0 + jnp.exp(-x))


def _const_spec(shape):
    return pl.BlockSpec(shape, lambda *_: (0,) * len(shape),
                        pipeline_mode=pl.Buffered(1))


MOD_TN = 1024


def _mod_kernel(c_ref, w_ref, b_ref, o_ref):
    c = c_ref[...]
    ca = c * _sigmoid(c)
    o_ref[...] = jnp.sum(ca * w_ref[...], axis=0, keepdims=True) + b_ref[...]


def _modulation(c, ada_w, ada_b):
    n = N_MOD * D_MODEL
    c_col = c.reshape(D_MODEL, 1)
    out = pl.pallas_call(
        _mod_kernel,
        grid=(DEPTH, n // MOD_TN),
        in_specs=[
            pl.BlockSpec((D_MODEL, 1), lambda l, j: (0, 0)),
            pl.BlockSpec((None, D_MODEL, MOD_TN), lambda l, j: (l, 0, j)),
            pl.BlockSpec((None, 1, MOD_TN), lambda l, j: (l, 0, j)),
        ],
        out_specs=pl.BlockSpec((None, 1, MOD_TN), lambda l, j: (l, 0, j)),
        out_shape=jax.ShapeDtypeStruct((DEPTH, 1, n), F32),
        compiler_params=pltpu.CompilerParams(
            dimension_semantics=("arbitrary", "arbitrary"),
            vmem_limit_bytes=VMEM_LIMIT),
        name="adaln_mod",
    )(c_col, ada_w, ada_b.reshape(DEPTH, 1, n))
    return out.reshape(DEPTH, N_MOD, D_MODEL)


def _norm_mod(x, g, shift, scale):
    ms = jnp.mean(x * x, axis=-1, keepdims=True)
    y = x * lax.rsqrt(ms + RMS_EPS) * g
    return y * (1.0 + scale) + shift


PROJ_TM = 512
ATT_NB = 4
ATT_GK = ATT_NB * MOBA_BLOCK
ATT_NG = SEQ // ATT_GK


def _proj_kernel(x_ref, mod_ref, g_ref, w_ref,
                 q_ref, k_ref, vt_ref, u_ref, km_ref, kn_ref):
    h = _norm_mod(x_ref[...], g_ref[...], mod_ref[0:1, :], mod_ref[1:2, :])
    h = h.astype(BF16)
    q_ref[...] = _dot(h, w_ref[:, :ATTN_W])
    k = _dot(h, w_ref[:, ATTN_W:2 * ATTN_W])
    k_ref[...] = k.astype(BF16)
    for bi in range(PROJ_TM // MOBA_BLOCK):
        kb = k[bi * MOBA_BLOCK:(bi + 1) * MOBA_BLOCK, :]
        km_ref[bi] = jnp.mean(kb, axis=0, keepdims=True)
        ksq = kb * kb
        norms = []
        for hd in range(N_HEADS):
            n2 = jnp.sum(ksq[:, hd * HEAD_DIM:(hd + 1) * HEAD_DIM], axis=1,
                         keepdims=True)
            norms.append(jnp.broadcast_to(
                jnp.max(n2, axis=0, keepdims=True), (1, HEAD_DIM)))
        kn_ref[bi] = jnp.concatenate(norms, axis=1)
    v = _dot(h, w_ref[:, 2 * ATTN_W:3 * ATTN_W])
    for hd in range(N_HEADS):
        vt_ref[hd, 0] = v[:, hd * HEAD_DIM:(hd + 1) * HEAD_DIM].T.astype(BF16)
    a = _dot(h, w_ref[:, 3 * ATTN_W:3 * ATTN_W + CONV_CH])
    g = _dot(h, w_ref[:, 3 * ATTN_W + CONV_CH:])
    u_ref[...] = a * _sigmoid(g)


def _in_proj(x, mod, pre_g, w):
    nbt = PROJ_TM // MOBA_BLOCK
    tpg = ATT_GK // PROJ_TM
    return pl.pallas_call(
        _proj_kernel,
        grid=(SEQ // PROJ_TM,),
        in_specs=[
            pl.BlockSpec((PROJ_TM, D_MODEL), lambda i: (i, 0)),
            _const_spec((N_MOD, D_MODEL)),
            _const_spec((1, D_MODEL)),
            _const_spec((D_MODEL, IN_COLS)),
        ],
        out_specs=[
            pl.BlockSpec((PROJ_TM, ATTN_W), lambda i: (i, 0)),
            pl.BlockSpec((PROJ_TM, ATTN_W), lambda i: (i, 0)),
            pl.BlockSpec((N_HEADS, 1, HEAD_DIM, PROJ_TM),
                         lambda i: (0, i // tpg, 0, i % tpg)),
            pl.BlockSpec((PROJ_TM, CONV_CH), lambda i: (i, 0)),
            pl.BlockSpec((nbt, 1, ATTN_W), lambda i: (i, 0, 0)),
            pl.BlockSpec((nbt, 1, ATTN_W), lambda i: (i, 0, 0)),
        ],
        out_shape=[
            jax.ShapeDtypeStruct((SEQ, ATTN_W), F32),
            jax.ShapeDtypeStruct((SEQ, ATTN_W), BF16),
            jax.ShapeDtypeStruct((N_HEADS, ATT_NG, HEAD_DIM, ATT_GK),
                                 BF16),
            jax.ShapeDtypeStruct((SEQ, CONV_CH), F32),
            jax.ShapeDtypeStruct((N_BLOCKS, 1, ATTN_W), F32),
            jax.ShapeDtypeStruct((N_BLOCKS, 1, ATTN_W), F32),
        ],
        compiler_params=pltpu.CompilerParams(
            dimension_semantics=("arbitrary",),
            vmem_limit_bytes=VMEM_LIMIT),
        name="in_proj",
    )(x, mod, pre_g, w)


ATT_QB = 2
ATT_TQ = ATT_QB * MOBA_BLOCK
ATT_G = 4
EXP2_SAFE = 80.0


def _split_bf16(x):
    hi = x.astype(BF16)
    lo = (x - hi.astype(F32)).astype(BF16)
    return hi, lo


def _block_bias(qh, ql, km, blk0, m_ref):
    kh, kl = _split_bf16(km)
    gate = _dot_nt(kh, qh) + (_dot_nt(kl, qh) + _dot_nt(kh, ql))
    cand = lax.broadcasted_iota(jnp.int32, (N_BLOCKS, ATT_TQ), 0)
    qcol = lax.broadcasted_iota(jnp.int32, (N_BLOCKS, ATT_TQ), 1)
    blk = blk0 + qcol // MOBA_BLOCK
    g = jnp.where(cand < blk, gate, M_INIT)
    bias = jnp.where(cand == blk, 0.0, MASKED)
    for _ in range(MOBA_TOPK):
        m = jnp.max(g, axis=0, keepdims=True)
        idx = jnp.min(jnp.where(g == m, cand, N_BLOCKS), axis=0, keepdims=True)
        hit = cand == idx
        bias = jnp.where(hit & (m > M_INIT), 0.0, bias)
        g = jnp.where(hit, -jnp.inf, g)
    trow = lax.broadcasted_iota(jnp.int32, (LANE - N_BLOCKS, ATT_TQ), 0)
    tail = jnp.where(trow == 0, -m_ref, 0.0)
    return jnp.concatenate([bias, tail], axis=0).T.astype(BF16)


def _attn_kernel(q_ref, k_ref, vt_ref, km_ref, kn_ref, o_ref, qaug_ref):
    tile = pl.program_id(1)
    blk0 = tile * ATT_QB
    score_sq_bound = jnp.float32(0.0)
    visible = lax.broadcasted_iota(
        jnp.int32, (N_BLOCKS, HEAD_DIM), 0) < blk0 + ATT_QB
    for h in range(ATT_G):
        hs = slice(h * HEAD_DIM, (h + 1) * HEAD_DIM)
        q32 = q_ref[:, hs]
        qn2 = jnp.max(jnp.sum(q32 * q32, axis=1, keepdims=True))
        kn2 = jnp.max(jnp.where(visible, kn_ref[:, hs], 0.0))
        score_sq_bound = jnp.maximum(score_sq_bound, qn2 * kn2)
        qh, ql = _split_bf16(q32)
        k_own = k_ref[pl.ds(pl.multiple_of(tile * ATT_TQ, ATT_TQ), ATT_TQ), hs]
        prod = (qh.astype(F32) * k_own.astype(F32)).astype(BF16)
        m_ref = _dot_nt(jnp.ones((16, HEAD_DIM), BF16), prod)[0:1, :]
        qaug_ref[h, :, :HEAD_DIM] = qh
        qaug_ref[h, :, HEAD_DIM:] = _block_bias(qh, ql, km_ref[:, hs], blk0,
                                                m_ref)

    def scores(g, h, causal):
        row0 = pl.multiple_of(g * ATT_GK, ATT_GK)
        key_blk = g * ATT_NB + lax.broadcasted_iota(
            jnp.int32, (ATT_GK, LANE), 0) // MOBA_BLOCK
        lane = lax.broadcasted_iota(jnp.int32, (ATT_GK, LANE), 1)
        extra = jnp.where((lane == key_blk) | (lane == N_BLOCKS),
                          1.0, 0.0).astype(BF16)
        kb = k_ref[pl.ds(row0, ATT_GK), h * HEAD_DIM:(h + 1) * HEAD_DIM]
        s = _dot_nt(jnp.concatenate([kb, extra], axis=1), qaug_ref[h])
        if causal:
            kpos = row0 + lax.broadcasted_iota(jnp.int32, (ATT_GK, ATT_TQ), 0)
            qpos = tile * ATT_TQ + lax.broadcasted_iota(
                jnp.int32, (ATT_GK, ATT_TQ), 1)
            s = jnp.where(kpos <= qpos, s, MASKED)
        return s

    def finish(carry):
        for h in range(ATT_G):
            psum, acc = carry[h]
            out = acc / jnp.sum(psum, axis=0, keepdims=True)
            o_ref[:, h * HEAD_DIM:(h + 1) * HEAD_DIM] = out.T.astype(BF16)

    def partial_rows(p):
        return jnp.sum(p.reshape(ATT_GK // SUBLANES, SUBLANES, ATT_TQ), axis=0)

    zeros = (jnp.zeros((SUBLANES, ATT_TQ), F32),
             jnp.zeros((HEAD_DIM, ATT_TQ), F32))
    n_full = blk0 // ATT_NB

    def run_bounded():
        def group(g, carry, causal):
            out = [None] * ATT_G
            held = None

            def pv(h, p):
                out[h] = (out[h], carry[h][1] + _dot(vt_ref[h, g], p))

            for h in range(ATT_G):
                p = jnp.exp2(scores(g, h, causal) * SCALE_LOG2E)
                out[h] = carry[h][0] + partial_rows(p)
                if held is not None:
                    pv(*held)
                held = (h, p.astype(BF16))
            pv(*held)
            return tuple(out)

        def pair(i, accs):
            return group(2 * i + 1, group(2 * i, accs, False), False)

        init = tuple(zeros for _ in range(ATT_G))
        accs = lax.fori_loop(0, n_full // 2, pair, init)
        odd = n_full % 2 == 1

        @pl.when(odd)
        def _():
            finish(group(n_full, group(n_full - 1, accs, False), True))

        @pl.when(jnp.logical_not(odd))
        def _():
            finish(group(n_full, accs, True))

    def run_unbounded():
        def group(g, carry, causal):
            out = []
            s_next = scores(g, 0, causal)
            for h in range(ATT_G):
                m, psum, acc = carry[h]
                s = s_next
                if h + 1 < ATT_G:
                    s_next = scores(g, h + 1, causal)
                m_new = jnp.maximum(m, jnp.max(s, axis=0, keepdims=True))
                p = jnp.exp2((s - m_new) * SCALE_LOG2E)
                alpha = jnp.exp2((m - m_new) * SCALE_LOG2E)
                out.append((m_new, alpha * psum + partial_rows(p),
                            alpha * acc + _dot(vt_ref[h, g], p.astype(BF16))))
            return tuple(out)

        init = tuple((jnp.full((1, ATT_TQ), M_INIT, F32),) + zeros
                     for _ in range(ATT_G))
        carry = lax.fori_loop(0, n_full, lambda g, c: group(g, c, False), init)
        carry = group(n_full, carry, True)
        finish([(psum, acc) for _, psum, acc in carry])

    limit = (0.5 * EXP2_SAFE / SCALE_LOG2E) ** 2
    bounded = score_sq_bound <= limit
    pl.when(bounded)(run_bounded)
    pl.when(jnp.logical_not(bounded))(run_unbounded)


def _moba_attention(q, k, vt, kmean, knorm):
    gw = ATT_G * HEAD_DIM
    return pl.pallas_call(
        _attn_kernel,
        grid=(N_HEADS // ATT_G, SEQ // ATT_TQ),
        in_specs=[
            pl.BlockSpec((ATT_TQ, gw), lambda h, t: (t, h)),
            pl.BlockSpec((SEQ, gw), lambda h, t: (0, h),
                         pipeline_mode=pl.Buffered(1)),
            pl.BlockSpec((ATT_G, ATT_NG, HEAD_DIM, ATT_GK),
                         lambda h, t: (h, 0, 0, 0),
                         pipeline_mode=pl.Buffered(1)),
            pl.BlockSpec((N_BLOCKS, gw), lambda h, t: (0, h)),
            pl.BlockSpec((N_BLOCKS, gw), lambda h, t: (0, h)),
        ],
        out_specs=pl.BlockSpec((ATT_TQ, gw), lambda h, t: (t, h)),
        out_shape=jax.ShapeDtypeStruct((SEQ, ATTN_W), BF16),
        scratch_shapes=[pltpu.VMEM((ATT_G, ATT_TQ, 2 * HEAD_DIM), BF16)],
        compiler_params=pltpu.CompilerParams(
            dimension_semantics=("arbitrary", "arbitrary"),
            vmem_limit_bytes=VMEM_LIMIT),
        name="moba_attn",
    )(q, k, vt, kmean, knorm)


CONV_TM = 512
CONV_HALO = 32
CONV_RC = 64
CONV_WIN = CONV_RC + CONV_HALO
LN_RC = 128
SUBLANES = 8


def _conv_kernel(u_ref, halo_ref, w_ref, b_ref, g_ref, beta_ref, o_ref,
                 buf_ref, y_ref):
    i = pl.program_id(0)
    halo = halo_ref[...]
    buf_ref[:CONV_HALO, :] = jnp.where(i > 0, halo, 0.0)
    buf_ref[CONV_HALO:, :] = u_ref[...]
    off = CONV_HALO - (CONV_K - 1)

    def conv_rows(r, _):
        base = pl.multiple_of(r * CONV_RC, CONV_RC)
        for c in range(CONV_CH // LANE):
            cs = slice(c * LANE, (c + 1) * LANE)
            win = buf_ref[pl.ds(base, CONV_WIN), cs]
            acc = jnp.zeros((CONV_RC, LANE), F32)
            for sub in range(SUBLANES):
                shifted = win if sub == 0 else pltpu.roll(
                    win, CONV_WIN - sub, axis=0)
                for j in range(CONV_K):
                    if (off + j) % SUBLANES != sub:
                        continue
                    lo = off + j - sub
                    acc = acc + w_ref[j:j + 1, cs] * shifted[lo:lo + CONV_RC, :]
            y_ref[pl.ds(base, CONV_RC), cs] = acc
        return 0

    lax.fori_loop(0, CONV_TM // CONV_RC, conv_rows, 0)

    def ln_rows(r, _):
        base = pl.multiple_of(r * LN_RC, LN_RC)
        y = y_ref[pl.ds(base, LN_RC), :] + b_ref[...]
        mu = jnp.mean(y, axis=-1, keepdims=True)
        yc = y - mu
        var = jnp.mean(yc * yc, axis=-1, keepdims=True)
        z = yc * lax.rsqrt(var + LN_EPS) * g_ref[...] + beta_ref[...]
        o_ref[pl.ds(base, LN_RC), :] = (z * _sigmoid(z)).astype(BF16)
        return 0

    lax.fori_loop(0, CONV_TM // LN_RC, ln_rows, 0)


def _conv_group(u, conv_w, conv_b, ln_g, ln_b):
    ratio = CONV_TM // CONV_HALO
    return pl.pallas_call(
        _conv_kernel,
        grid=(SEQ // CONV_TM,),
        in_specs=[
            pl.BlockSpec((CONV_TM, CONV_CH), lambda i: (i, 0)),
            pl.BlockSpec((CONV_HALO, CONV_CH),
                         lambda i: (jnp.maximum(i * ratio - 1, 0), 0)),
            _const_spec((CONV_K, CONV_CH)),
            _const_spec((1, CONV_CH)),
            _const_spec((1, CONV_CH)),
            _const_spec((1, CONV_CH)),
        ],
        out_specs=pl.BlockSpec((CONV_TM, CONV_CH), lambda i: (i, 0)),
        out_shape=jax.ShapeDtypeStruct((SEQ, CONV_CH), BF16),
        scratch_shapes=[pltpu.VMEM((CONV_TM + CONV_HALO, CONV_CH), F32),
                        pltpu.VMEM((CONV_TM, CONV_CH), F32)],
        compiler_params=pltpu.CompilerParams(
            dimension_semantics=("arbitrary",),
            vmem_limit_bytes=VMEM_LIMIT),
        name="conv_group",
    )(u, u, conv_w, conv_b, ln_g, ln_b)


OUT_TM = 512


def _rms(y, g):
    ms = jnp.mean(y * y, axis=-1, keepdims=True)
    return y * lax.rsqrt(ms + RMS_EPS) * g


def _out_kernel(x_ref, a_ref, c_ref, w_ref, mod_ref, g_ref, o_ref):
    y = _dot(a_ref[...], w_ref[:ATTN_W, :]) + _dot(c_ref[...], w_ref[ATTN_W:, :])
    o_ref[...] = x_ref[...] + mod_ref[2:3, :] * _rms(y, g_ref[...])


def _out_proj(x, attn, conv, w_out, mod, post_g):
    return pl.pallas_call(
        _out_kernel,
        grid=(SEQ // OUT_TM,),
        in_specs=[
            pl.BlockSpec((OUT_TM, D_MODEL), lambda i: (i, 0)),
            pl.BlockSpec((OUT_TM, ATTN_W), lambda i: (i, 0)),
            pl.BlockSpec((OUT_TM, CONV_CH), lambda i: (i, 0)),
            _const_spec((ATTN_W + CONV_CH, D_MODEL)),
            _const_spec((N_MOD, D_MODEL)),
            _const_spec((1, D_MODEL)),
        ],
        out_specs=pl.BlockSpec((OUT_TM, D_MODEL), lambda i: (i, 0)),
        out_shape=jax.ShapeDtypeStruct((SEQ, D_MODEL), F32),
        compiler_params=pltpu.CompilerParams(
            dimension_semantics=("arbitrary",),
            vmem_limit_bytes=VMEM_LIMIT),
        name="out_proj",
    )(x, attn, conv, w_out, mod, post_g)


FFN_TM = 512
FFN_TF = 512


def _ffn_kernel(x_ref, mod_ref, pre_ref, post_ref, wg_ref, wu_ref, wd_ref,
                o_ref, h_ref, acc_ref):
    f = pl.program_id(1)
    last = pl.num_programs(1) - 1

    def chunk(h):
        g = _dot(h, wg_ref[...])
        u = _dot(h, wu_ref[...])
        a = (g * _sigmoid(g) * u).astype(BF16)
        return _dot(a, wd_ref[...])

    @pl.when(f == 0)
    def _():
        h = _norm_mod(x_ref[...], pre_ref[...], mod_ref[3:4, :], mod_ref[4:5, :])
        h = h.astype(BF16)
        h_ref[...] = h
        acc_ref[...] = chunk(h)

    @pl.when((f > 0) & (f < last))
    def _():
        acc_ref[...] += chunk(h_ref[...])

    @pl.when(f == last)
    def _():
        y = acc_ref[...] + chunk(h_ref[...])
        o_ref[...] = x_ref[...] + mod_ref[5:6, :] * _rms(y, post_ref[...])


def _ffn(x, mod, pre_g, post_g, w_gate, w_up, w_down):
    return pl.pallas_call(
        _ffn_kernel,
        grid=(SEQ // FFN_TM, D_FF // FFN_TF),
        in_specs=[
            pl.BlockSpec((FFN_TM, D_MODEL), lambda i, f: (i, 0)),
            _const_spec((N_MOD, D_MODEL)),
            _const_spec((1, D_MODEL)),
            _const_spec((1, D_MODEL)),
            pl.BlockSpec((D_MODEL, FFN_TF), lambda i, f: (0, f)),
            pl.BlockSpec((D_MODEL, FFN_TF), lambda i, f: (0, f)),
            pl.BlockSpec((FFN_TF, D_MODEL), lambda i, f: (f, 0)),
        ],
        out_specs=pl.BlockSpec((FFN_TM, D_MODEL), lambda i, f: (i, 0)),
        out_shape=jax.ShapeDtypeStruct((SEQ, D_MODEL), F32),
        scratch_shapes=[pltpu.VMEM((FFN_TM, D_MODEL), BF16),
                        pltpu.VMEM((FFN_TM, D_MODEL), F32)],
        compiler_params=pltpu.CompilerParams(
            dimension_semantics=("arbitrary", "arbitrary"),
            vmem_limit_bytes=VMEM_LIMIT),
        name="ffn",
    )(x, mod, pre_g, post_g, w_gate, w_up, w_down)


def kernel(x, c, ada_w, ada_b, mix_pre_g, mix_post_g, w_in, conv_w, conv_b,
           conv_ln_g, conv_ln_b, w_out, ffn_pre_g, ffn_post_g, w_gate, w_up,
           w_down):
    assert x.shape == (1, SEQ, D_MODEL) and c.shape == (1, D_MODEL)
    xs = x.reshape(SEQ, D_MODEL)
    mod = _modulation(c, ada_w, ada_b)
    row = lambda v: v.reshape(1, -1)
    for l in range(DEPTH):
        q, k, vt, u, kmean, knorm = _in_proj(xs, mod[l], row(mix_pre_g[l]),
                                             w_in[l].astype(BF16))
        attn = _moba_attention(q, k, vt, kmean.reshape(N_BLOCKS, ATTN_W),
                               knorm.reshape(N_BLOCKS, ATTN_W))
        conv = _conv_group(u, conv_w[l], row(conv_b[l]), row(conv_ln_g[l]),
                           row(conv_ln_b[l]))
        xs = _out_proj(xs, attn, conv, w_out[l].astype(BF16), mod[l],
                       row(mix_post_g[l]))
        xs = _ffn(xs, mod[l], row(ffn_pre_g[l]), row(ffn_post_g[l]),
                  w_gate[l].astype(BF16), w_up[l].astype(BF16),
                  w_down[l].astype(BF16))
    return xs.reshape(1, SEQ, D_MODEL)
```

```python
import functools
import math

import jax
import jax.numpy as jnp
from jax import lax
from jax.experimental import pallas as pl
from jax.experimental.pallas import tpu as pltpu

D_MODEL = 2048
SEQ = 16384
DEPTH = 4
HEAD_DIM = 128
N_HEADS = 8
ATTN_W = N_HEADS * HEAD_DIM
CONV_CH = 1024
IN_COLS = 3 * ATTN_W + 2 * CONV_CH
CONV_K = 31
MOBA_BLOCK = 256
MOBA_TOPK = 3
N_BLOCKS = SEQ // MOBA_BLOCK
D_FF = 5632
N_MOD = 6
RMS_EPS = 1e-6
LN_EPS = 1e-5

M_INIT = -1e30
MASKED = -2e30

LANE = 128
VMEM_LIMIT = 60 * 1024 * 1024

F32 = jnp.float32
BF16 = jnp.bfloat16

SCALE_LOG2E = (HEAD_DIM ** -0.5) * math.log2(math.e)


def _dot(a, b):
    return jnp.dot(a, b, preferred_element_type=F32)


def _dot_nt(a, b):
    return lax.dot_general(a, b, (((1,), (1,)), ((), ())),
                           preferred_element_type=F32)


def _sigmoid(x):
    return 1.0 / (1.0 + jnp.exp(-x))


def _const_spec(shape):
    return pl.BlockSpec(shape, lambda *_: (0,) * len(shape),
                        pipeline_mode=pl.Buffered(1))


MOD_TN = 1024


def _mod_kernel(c_ref, w_ref, b_ref, o_ref):
    c = c_ref[...]
    ca = c * _sigmoid(c)
    o_ref[...] = jnp.sum(ca * w_ref[...], axis=0, keepdims=True) + b_ref[...]


def _modulation(c, ada_w, ada_b):
    n = N_MOD * D_MODEL
    c_col = c.reshape(D_MODEL, 1)
    out = pl.pallas_call(
        _mod_kernel,
        grid=(DEPTH, n // MOD_TN),
        in_specs=[
            pl.BlockSpec((D_MODEL, 1), lambda l, j: (0, 0)),
            pl.BlockSpec((None, D_MODEL, MOD_TN), lambda l, j: (l, 0, j)),
            pl.BlockSpec((None, 1, MOD_TN), lambda l, j: (l, 0, j)),
        ],
        out_specs=pl.BlockSpec((None, 1, MOD_TN), lambda l, j: (l, 0, j)),
        out_shape=jax.ShapeDtypeStruct((DEPTH, 1, n), F32),
        compiler_params=pltpu.CompilerParams(
            dimension_semantics=("arbitrary", "arbitrary"),
            vmem_limit_bytes=VMEM_LIMIT),
        name="adaln_mod",
    )(c_col, ada_w, ada_b.reshape(DEPTH, 1, n))
    return out.reshape(DEPTH, N_MOD, D_MODEL)


def _norm_mod(x, g, shift, scale):
    ms = jnp.mean(x * x, axis=-1, keepdims=True)
    y = x * lax.rsqrt(ms + RMS_EPS) * g
    return y * (1.0 + scale) + shift


PROJ_TM = 512
ATT_NB = 4
ATT_GK = ATT_NB * MOBA_BLOCK
ATT_NG = SEQ // ATT_GK


def _proj_kernel(x_ref, mod_ref, g_ref, w_ref,
                 q_ref, k_ref, vt_ref, u_ref, km_ref, kn_ref):
    h = _norm_mod(x_ref[...], g_ref[...], mod_ref[0:1, :], mod_ref[1:2, :])
    h = h.astype(BF16)
    q_ref[...] = _dot(h, w_ref[:, :ATTN_W])
    k = _dot(h, w_ref[:, ATTN_W:2 * ATTN_W])
    k_ref[...] = k.astype(BF16)
    for bi in range(PROJ_TM // MOBA_BLOCK):
        kb = k[bi * MOBA_BLOCK:(bi + 1) * MOBA_BLOCK, :]
        km_ref[bi] = jnp.mean(kb, axis=0, keepdims=True)
        ksq = kb * kb
        norms = []
        for hd in range(N_HEADS):
            n2 = jnp.sum(ksq[:, hd * HEAD_DIM:(hd + 1) * HEAD_DIM], axis=1,
                         keepdims=True)
            norms.append(jnp.broadcast_to(
                jnp.max(n2, axis=0, keepdims=True), (1, HEAD_DIM)))
        kn_ref[bi] = jnp.concatenate(norms, axis=1)
    v = _dot(h, w_ref[:, 2 * ATTN_W:3 * ATTN_W])
    for hd in range(N_HEADS):
        vt_ref[hd, 0] = v[:, hd * HEAD_DIM:(hd + 1) * HEAD_DIM].T.astype(BF16)
    a = _dot(h, w_ref[:, 3 * ATTN_W:3 * ATTN_W + CONV_CH])
    g = _dot(h, w_ref[:, 3 * ATTN_W + CONV_CH:])
    u_ref[...] = a * _sigmoid(g)


def _in_proj(x, mod, pre_g, w):
    nbt = PROJ_TM // MOBA_BLOCK
    tpg = ATT_GK // PROJ_TM
    return pl.pallas_call(
        _proj_kernel,
        grid=(SEQ // PROJ_TM,),
        in_specs=[
            pl.BlockSpec((PROJ_TM, D_MODEL), lambda i: (i, 0)),
            _const_spec((N_MOD, D_MODEL)),
            _const_spec((1, D_MODEL)),
            _const_spec((D_MODEL, IN_COLS)),
        ],
        out_specs=[
            pl.BlockSpec((PROJ_TM, ATTN_W), lambda i: (i, 0)),
            pl.BlockSpec((PROJ_TM, ATTN_W), lambda i: (i, 0)),
            pl.BlockSpec((N_HEADS, 1, HEAD_DIM, PROJ_TM),
                         lambda i: (0, i // tpg, 0, i % tpg)),
            pl.BlockSpec((PROJ_TM, CONV_CH), lambda i: (i, 0)),
            pl.BlockSpec((nbt, 1, ATTN_W), lambda i: (i, 0, 0)),
            pl.BlockSpec((nbt, 1, ATTN_W), lambda i: (i, 0, 0)),
        ],
        out_shape=[
            jax.ShapeDtypeStruct((SEQ, ATTN_W), F32),
            jax.ShapeDtypeStruct((SEQ, ATTN_W), BF16),
            jax.ShapeDtypeStruct((N_HEADS, ATT_NG, HEAD_DIM, ATT_GK),
                                 BF16),
            jax.ShapeDtypeStruct((SEQ, CONV_CH), F32),
            jax.ShapeDtypeStruct((N_BLOCKS, 1, ATTN_W), F32),
            jax.ShapeDtypeStruct((N_BLOCKS, 1, ATTN_W), F32),
        ],
        compiler_params=pltpu.CompilerParams(
            dimension_semantics=("arbitrary",),
            vmem_limit_bytes=VMEM_LIMIT),
        name="in_proj",
    )(x, mod, pre_g, w)


ATT_QB = 2
ATT_TQ = ATT_QB * MOBA_BLOCK
ATT_G = 4
EXP2_SAFE = 80.0


def _split_bf16(x):
    hi = x.astype(BF16)
    lo = (x - hi.astype(F32)).astype(BF16)
    return hi, lo


def _block_bias(qh, ql, km, blk0, m_ref):
    kh, kl = _split_bf16(km)
    gate = _dot_nt(kh, qh) + (_dot_nt(kl, qh) + _dot_nt(kh, ql))
    cand = lax.broadcasted_iota(jnp.int32, (N_BLOCKS, ATT_TQ), 0)
    qcol = lax.broadcasted_iota(jnp.int32, (N_BLOCKS, ATT_TQ), 1)
    blk = blk0 + qcol // MOBA_BLOCK
    g = jnp.where(cand < blk, gate, M_INIT)
    bias = jnp.where(cand == blk, 0.0, MASKED)
    for _ in range(MOBA_TOPK):
        m = jnp.max(g, axis=0, keepdims=True)
        idx = jnp.min(jnp.where(g == m, cand, N_BLOCKS), axis=0, keepdims=True)
        hit = cand == idx
        bias = jnp.where(hit & (m > M_INIT), 0.0, bias)
        g = jnp.where(hit, -jnp.inf, g)
    trow = lax.broadcasted_iota(jnp.int32, (LANE - N_BLOCKS, ATT_TQ), 0)
    tail = jnp.where(trow == 0, -m_ref, 0.0)
    return jnp.concatenate([bias, tail], axis=0).T.astype(BF16)


def _attn_kernel(q_ref, k_ref, vt_ref, km_ref, kn_ref, o_ref, qaug_ref):
    tile = pl.program_id(1)
    blk0 = tile * ATT_QB
    score_sq_bound = jnp.float32(0.0)
    visible = lax.broadcasted_iota(
        jnp.int32, (N_BLOCKS, HEAD_DIM), 0) < blk0 + ATT_QB
    for h in range(ATT_G):
        hs = slice(h * HEAD_DIM, (h + 1) * HEAD_DIM)
        q32 = q_ref[:, hs]
        qn2 = jnp.max(jnp.sum(q32 * q32, axis=1, keepdims=True))
        kn2 = jnp.max(jnp.where(visible, kn_ref[:, hs], 0.0))
        score_sq_bound = jnp.maximum(score_sq_bound, qn2 * kn2)
        qh, ql = _split_bf16(q32)
        k_own = k_ref[pl.ds(pl.multiple_of(tile * ATT_TQ, ATT_TQ), ATT_TQ), hs]
        prod = (qh.astype(F32) * k_own.astype(F32)).astype(BF16)
        m_ref = _dot_nt(jnp.ones((16, HEAD_DIM), BF16), prod)[0:1, :]
        qaug_ref[h, :, :HEAD_DIM] = qh
        qaug_ref[h, :, HEAD_DIM:] = _block_bias(qh, ql, km_ref[:, hs], blk0,
                                                m_ref)

    def scores(g, h, causal):
        row0 = pl.multiple_of(g * ATT_GK, ATT_GK)
        key_blk = g * ATT_NB + lax.broadcasted_iota(
            jnp.int32, (ATT_GK, LANE), 0) // MOBA_BLOCK
        lane = lax.broadcasted_iota(jnp.int32, (ATT_GK, LANE), 1)
        extra = jnp.where((lane == key_blk) | (lane == N_BLOCKS),
                          1.0, 0.0).astype(BF16)
        kb = k_ref[pl.ds(row0, ATT_GK), h * HEAD_DIM:(h + 1) * HEAD_DIM]
        s = _dot_nt(jnp.concatenate([kb, extra], axis=1), qaug_ref[h])
        if causal:
            kpos = row0 + lax.broadcasted_iota(jnp.int32, (ATT_GK, ATT_TQ), 0)
            qpos = tile * ATT_TQ + lax.broadcasted_iota(
                jnp.int32, (ATT_GK, ATT_TQ), 1)
            s = jnp.where(kpos <= qpos, s, MASKED)
        return s

    def finish(carry):
        for h in range(ATT_G):
            psum, acc = carry[h]
            out = acc / jnp.sum(psum, axis=0, keepdims=True)
            o_ref[:, h * HEAD_DIM:(h + 1) * HEAD_DIM] = out.T.astype(BF16)

    def partial_rows(p):
        return jnp.sum(p.reshape(ATT_GK // SUBLANES, SUBLANES, ATT_TQ), axis=0)

    zeros = (jnp.zeros((SUBLANES, ATT_TQ), F32),
             jnp.zeros((HEAD_DIM, ATT_TQ), F32))
    n_full = blk0 // ATT_NB

    def run_bounded():
        def group(g, carry, causal):
            out = [None] * ATT_G
            held = None

            def pv(h, p):
                out[h] = (out[h], carry[h][1] + _dot(vt_ref[h, g], p))

            for h in range(ATT_G):
                p = jnp.exp2(scores(g, h, causal) * SCALE_LOG2E)
                out[h] = carry[h][0] + partial_rows(p)
                if held is not None:
                    pv(*held)
                held = (h, p.astype(BF16))
            pv(*held)
            return tuple(out)

        def pair(i, accs):
            return group(2 * i + 1, group(2 * i, accs, False), False)

        init = tuple(zeros for _ in range(ATT_G))
        accs = lax.fori_loop(0, n_full // 2, pair, init)
        odd = n_full % 2 == 1

        @pl.when(odd)
        def _():
            finish(group(n_full, group(n_full - 1, accs, False), True))

        @pl.when(jnp.logical_not(odd))
        def _():
            finish(group(n_full, accs, True))

    def run_unbounded():
        def group(g, carry, causal):
            out = []
            s_next = scores(g, 0, causal)
            for h in range(ATT_G):
                m, psum, acc = carry[h]
                s = s_next
                if h + 1 < ATT_G:
                    s_next = scores(g, h + 1, causal)
                m_new = jnp.maximum(m, jnp.max(s, axis=0, keepdims=True))
                p = jnp.exp2((s - m_new) * SCALE_LOG2E)
                alpha = jnp.exp2((m - m_new) * SCALE_LOG2E)
                out.append((m_new, alpha * psum + partial_rows(p),
                            alpha * acc + _dot(vt_ref[h, g], p.astype(BF16))))
            return tuple(out)

        init = tuple((jnp.full((1, ATT_TQ), M_INIT, F32),) + zeros
                     for _ in range(ATT_G))
        carry = lax.fori_loop(0, n_full, lambda g, c: group(g, c, False), init)
        carry = group(n_full, carry, True)
        finish([(psum, acc) for _, psum, acc in carry])

    limit = (0.5 * EXP2_SAFE / SCALE_LOG2E) ** 2
    bounded = score_sq_bound <= limit
    pl.when(bounded)(run_bounded)
    pl.when(jnp.logical_not(bounded))(run_unbounded)


def _moba_attention(q, k, vt, kmean, knorm):
    gw = ATT_G * HEAD_DIM
    return pl.pallas_call(
        _attn_kernel,
        grid=(N_HEADS // ATT_G, SEQ // ATT_TQ),
        in_specs=[
            pl.BlockSpec((ATT_TQ, gw), lambda h, t: (t, h)),
            pl.BlockSpec((SEQ, gw), lambda h, t: (0, h),
                         pipeline_mode=pl.Buffered(1)),
            pl.BlockSpec((ATT_G, ATT_NG, HEAD_DIM, ATT_GK),
                         lambda h, t: (h, 0, 0, 0),
                         pipeline_mode=pl.Buffered(1)),
            pl.BlockSpec((N_BLOCKS, gw), lambda h, t: (0, h)),
            pl.BlockSpec((N_BLOCKS, gw), lambda h, t: (0, h)),
        ],
        out_specs=pl.BlockSpec((ATT_TQ, gw), lambda h, t: (t, h)),
        out_shape=jax.ShapeDtypeStruct((SEQ, ATTN_W), BF16),
        scratch_shapes=[pltpu.VMEM((ATT_G, ATT_TQ, 2 * HEAD_DIM), BF16)],
        compiler_params=pltpu.CompilerParams(
            dimension_semantics=("arbitrary", "arbitrary"),
            vmem_limit_bytes=VMEM_LIMIT),
        name="moba_attn",
    )(q, k, vt, kmean, knorm)


CONV_TM = 512
CONV_HALO = 32
CONV_RC = 64
CONV_WIN = CONV_RC + CONV_HALO
LN_RC = 128
SUBLANES = 8


def _conv_kernel(u_ref, halo_ref, w_ref, b_ref, g_ref, beta_ref, o_ref,
                 buf_ref, y_ref):
    i = pl.program_id(0)
    halo = halo_ref[...]
    buf_ref[:CONV_HALO, :] = jnp.where(i > 0, halo, 0.0)
    buf_ref[CONV_HALO:, :] = u_ref[...]
    off = CONV_HALO - (CONV_K - 1)

    def conv_rows(r, _):
        base = pl.multiple_of(r * CONV_RC, CONV_RC)
        for c in range(CONV_CH // LANE):
            cs = slice(c * LANE, (c + 1) * LANE)
            win = buf_ref[pl.ds(base, CONV_WIN), cs]
            acc = jnp.zeros((CONV_RC, LANE), F32)
            for sub in range(SUBLANES):
                shifted = win if sub == 0 else pltpu.roll(
                    win, CONV_WIN - sub, axis=0)
                for j in range(CONV_K):
                    if (off + j) % SUBLANES != sub:
                        continue
                    lo = off + j - sub
                    acc = acc + w_ref[j:j + 1, cs] * shifted[lo:lo + CONV_RC, :]
            y_ref[pl.ds(base, CONV_RC), cs] = acc
        return 0

    lax.fori_loop(0, CONV_TM // CONV_RC, conv_rows, 0)

    def ln_rows(r, _):
        base = pl.multiple_of(r * LN_RC, LN_RC)
        y = y_ref[pl.ds(base, LN_RC), :] + b_ref[...]
        mu = jnp.mean(y, axis=-1, keepdims=True)
        yc = y - mu
        var = jnp.mean(yc * yc, axis=-1, keepdims=True)
        z = yc * lax.rsqrt(var + LN_EPS) * g_ref[...] + beta_ref[...]
        o_ref[pl.ds(base, LN_RC), :] = (z * _sigmoid(z)).astype(BF16)
        return 0

    lax.fori_loop(0, CONV_TM // LN_RC, ln_rows, 0)


def _conv_group(u, conv_w, conv_b, ln_g, ln_b):
    ratio = CONV_TM // CONV_HALO
    return pl.pallas_call(
        _conv_kernel,
        grid=(SEQ // CONV_TM,),
        in_specs=[
            pl.BlockSpec((CONV_TM, CONV_CH), lambda i: (i, 0)),
            pl.BlockSpec((CONV_HALO, CONV_CH),
                         lambda i: (jnp.maximum(i * ratio - 1, 0), 0)),
            _const_spec((CONV_K, CONV_CH)),
            _const_spec((1, CONV_CH)),
            _const_spec((1, CONV_CH)),
            _const_spec((1, CONV_CH)),
        ],
        out_specs=pl.BlockSpec((CONV_TM, CONV_CH), lambda i: (i, 0)),
        out_shape=jax.ShapeDtypeStruct((SEQ, CONV_CH), BF16),
        scratch_shapes=[pltpu.VMEM((CONV_TM + CONV_HALO, CONV_CH), F32),
                        pltpu.VMEM((CONV_TM, CONV_CH), F32)],
        compiler_params=pltpu.CompilerParams(
            dimension_semantics=("arbitrary",),
            vmem_limit_bytes=VMEM_LIMIT),
        name="conv_group",
    )(u, u, conv_w, conv_b, ln_g, ln_b)


OUT_TM = 512


def _rms(y, g):
    ms = jnp.mean(y * y, axis=-1, keepdims=True)
    return y * lax.rsqrt(ms + RMS_EPS) * g


def _out_kernel(x_ref, a_ref, c_ref, w_ref, mod_ref, g_ref, o_ref):
    y = _dot(a_ref[...], w_ref[:ATTN_W, :]) + _dot(c_ref[...], w_ref[ATTN_W:, :])
    o_ref[...] = x_ref[...] + mod_ref[2:3, :] * _rms(y, g_ref[...])


def _out_proj(x, attn, conv, w_out, mod, post_g):
    return pl.pallas_call(
        _out_kernel,
        grid=(SEQ // OUT_TM,),
        in_specs=[
            pl.BlockSpec((OUT_TM, D_MODEL), lambda i: (i, 0)),
            pl.BlockSpec((OUT_TM, ATTN_W), lambda i: (i, 0)),
            pl.BlockSpec((OUT_TM, CONV_CH), lambda i: (i, 0)),
            _const_spec((ATTN_W + CONV_CH, D_MODEL)),
            _const_spec((N_MOD, D_MODEL)),
            _const_spec((1, D_MODEL)),
        ],
        out_specs=pl.BlockSpec((OUT_TM, D_MODEL), lambda i: (i, 0)),
        out_shape=jax.ShapeDtypeStruct((SEQ, D_MODEL), F32),
        compiler_params=pltpu.CompilerParams(
            dimension_semantics=("arbitrary",),
            vmem_limit_bytes=VMEM_LIMIT),
        name="out_proj",
    )(x, attn, conv, w_out, mod, post_g)


FFN_TM = 1024
FFN_TF = 512


def _ffn_kernel(x_ref, mod_ref, pre_ref, post_ref, wg_ref, wu_ref, wd_ref,
                o_ref, h_ref):
    acc_ref = o_ref
    f = pl.program_id(1)
    last = pl.num_programs(1) - 1

    def chunk(h):
        g = _dot(h, wg_ref[...])
        u = _dot(h, wu_ref[...])
        a = (g * _sigmoid(g) * u).astype(BF16)
        return _dot(a, wd_ref[...])

    @pl.when(f == 0)
    def _():
        h = _norm_mod(x_ref[...], pre_ref[...], mod_ref[3:4, :], mod_ref[4:5, :])
        h = h.astype(BF16)
        h_ref[...] = h
        acc_ref[...] = chunk(h)

    @pl.when((f > 0) & (f < last))
    def _():
        acc_ref[...] += chunk(h_ref[...])

    @pl.when(f == last)
    def _():
        y = acc_ref[...] + chunk(h_ref[...])
        o_ref[...] = x_ref[...] + mod_ref[5:6, :] * _rms(y, post_ref[...])


def _ffn(x, mod, pre_g, post_g, w_gate, w_up, w_down):
    return pl.pallas_call(
        _ffn_kernel,
        grid=(SEQ // FFN_TM, D_FF // FFN_TF),
        in_specs=[
            pl.BlockSpec((FFN_TM, D_MODEL), lambda i, f: (i, 0)),
            _const_spec((N_MOD, D_MODEL)),
            _const_spec((1, D_MODEL)),
            _const_spec((1, D_MODEL)),
            pl.BlockSpec((D_MODEL, FFN_TF), lambda i, f: (0, f)),
            pl.BlockSpec((D_MODEL, FFN_TF), lambda i, f: (0, f)),
            pl.BlockSpec((FFN_TF, D_MODEL), lambda i, f: (f, 0)),
        ],
        out_specs=pl.BlockSpec((FFN_TM, D_MODEL), lambda i, f: (i, 0)),
        out_shape=jax.ShapeDtypeStruct((SEQ, D_MODEL), F32),
        scratch_shapes=[pltpu.VMEM((FFN_TM, D_MODEL), BF16)],
        compiler_params=pltpu.CompilerParams(
            dimension_semantics=("arbitrary", "arbitrary"),
            vmem_limit_bytes=VMEM_LIMIT),
        name="ffn",
    )(x, mod, pre_g, post_g, w_gate, w_up, w_down)


def kernel(x, c, ada_w, ada_b, mix_pre_g, mix_post_g, w_in, conv_w, conv_b,
           conv_ln_g, conv_ln_b, w_out, ffn_pre_g, ffn_post_g, w_gate, w_up,
           w_down):
    assert x.shape == (1, SEQ, D_MODEL) and c.shape == (1, D_MODEL)
    xs = x.reshape(SEQ, D_MODEL)
    mod = _modulation(c, ada_w, ada_b)
    row = lambda v: v.reshape(1, -1)
    for l in range(DEPTH):
        q, k, vt, u, kmean, knorm = _in_proj(xs, mod[l], row(mix_pre_g[l]),
                                             w_in[l].astype(BF16))
        attn = _moba_attention(q, k, vt, kmean.reshape(N_BLOCKS, ATTN_W),
                               knorm.reshape(N_BLOCKS, ATTN_W))
        conv = _conv_group(u, conv_w[l], row(conv_b[l]), row(conv_ln_g[l]),
                           row(conv_ln_b[l]))
        xs = _out_proj(xs, attn, conv, w_out[l].astype(BF16), mod[l],
                       row(mix_post_g[l]))
        xs = _ffn(xs, mod[l], row(ffn_pre_g[l]), row(ffn_post_g[l]),
                  w_gate[l].astype(BF16), w_up[l].astype(BF16),
                  w_down[l].astype(BF16))
    return xs.reshape(1, SEQ, D_MODEL)
```
